```python
import math
import jax, jax.numpy as jnp
from jax import lax
import numpy as np

D_MODEL = 1024
BATCH = 2
SEQ = 8192
DEPTH = 4
DEC_BATCH = 128
DEC_SEQ = 1
PAST_LEN = 2048
PAGE_SIZE = 128

N_MIXERS = 3
N_A = (DEPTH + 2) // 3
N_B = (DEPTH + 1) // 3
N_C = DEPTH // 3
D_A = 64
H_A = D_MODEL // (2 * D_A)
E_A = H_A * 2 * D_A
D_B = 64
H_B = D_MODEL // D_B
E_B = H_B * D_B
E_C = D_MODEL
CH_C = 16
G_C = E_C // CH_C
P_C = 64
Q_BLOCK = 128
ROPE_THETA = 10000.0
NORM_EPS = 1e-6
SUBLN_EPS = 1e-5
NEG_INF = -1e30
F32 = jnp.float32

kernel_name = 'hybrid_diff_fox_s5_step'


def rms_norm(x, g, eps=NORM_EPS):
    xf = x.astype(F32)
    y = xf * lax.rsqrt(jnp.mean(xf * xf, axis=-1, keepdims=True) + eps)
    return (y * g.astype(F32)).astype(x.dtype)


def rope(x, pos):
    d = x.shape[-1]
    half = d // 2
    inv = ROPE_THETA ** (-jnp.arange(half, dtype=F32) / half)
    ang = pos.astype(F32)[:, None] * inv[None, :]
    shape = (1, ang.shape[0]) + (1,) * (x.ndim - 3) + (half,)
    cos = jnp.cos(ang).reshape(shape)
    sin = jnp.sin(ang).reshape(shape)
    xf = x.astype(F32)
    x1, x2 = xf[..., :half], xf[..., half:]
    return jnp.concatenate([x1 * cos - x2 * sin, x2 * cos + x1 * sin], axis=-1).astype(x.dtype)


def diff_lambda_init(layer_idx):
    return 0.8 - 0.6 * math.exp(-0.3 * layer_idx)


def diff_project(h, w_in, pos):
    b, l, _ = h.shape
    proj = h @ w_in
    q = rope(proj[..., :E_A].reshape(b, l, H_A, 2, D_A), pos)
    k = rope(proj[..., E_A:2 * E_A].reshape(b, l, H_A, 2, D_A), pos)
    v = proj[..., 2 * E_A:3 * E_A].reshape(b, l, H_A, 2 * D_A)
    z = proj[..., 3 * E_A:]
    return q, k, v, z


def diff_attn_prompt(q, k, v, lam):
    b, l = q.shape[:2]
    nb = l // Q_BLOCK
    scale = D_A ** -0.5
    qb = jnp.moveaxis(q.reshape(b, nb, Q_BLOCK, H_A, 2, D_A), 1, 0)
    kf, vf = k.astype(F32), v.astype(F32)
    kpos = jnp.arange(l)

    def block(args):
        qi, i = args
        s = jnp.einsum('bqhcd,bkhcd->bhcqk', qi.astype(F32), kf) * scale
        qpos = i * Q_BLOCK + jnp.arange(Q_BLOCK)
        s = jnp.where(kpos[None, :] <= qpos[:, None], s, NEG_INF)
        a = jax.nn.softmax(s, axis=-1)
        w = a[:, :, 0] - lam * a[:, :, 1]
        return jnp.einsum('bhqk,bkhe->bqhe', w, vf)

    o = lax.map(block, (qb, jnp.arange(nb)))
    return jnp.moveaxis(o, 0, 1).reshape(b, l, H_A, 2 * D_A)


def diff_attn_sample(q, k_new, v_new, k_past, v_past, lam):
    scale = D_A ** -0.5
    t = q.shape[1]
    p = k_past.shape[1]
    qf = q.astype(F32)
    s_past = jnp.einsum('bqhcd,bkhcd->bhcqk', qf, k_past.astype(F32)) * scale
    s_new = jnp.einsum('bqhcd,bkhcd->bhcqk', qf, k_new.astype(F32)) * scale
    s_new = jnp.where(jnp.tril(jnp.ones((t, t), bool)), s_new, NEG_INF)
    a = jax.nn.softmax(jnp.concatenate([s_past, s_new], axis=-1), axis=-1)
    w = a[:, :, 0] - lam * a[:, :, 1]
    return (jnp.einsum('bhqk,bkhe->bqhe', w[..., :p], v_past.astype(F32))
            + jnp.einsum('bhqk,bkhe->bqhe', w[..., p:], v_new.astype(F32)))


def diff_output(o, z, subln_g, w_out, lam_init, dtype):
    b, l = o.shape[:2]
    o = rms_norm(o, subln_g, SUBLN_EPS) * (1.0 - lam_init)
    o = o.reshape(b, l, E_A).astype(dtype) * jax.nn.silu(z)
    return (o @ w_out).astype(dtype)


def diff_layer(hp, hs, k_pool, v_pool, j, page_table, w_in, lam_vec, subln_g, w_out, lam_init):
    lv = lam_vec.astype(F32)
    lam = jnp.exp(jnp.sum(lv[0] * lv[1])) - jnp.exp(jnp.sum(lv[2] * lv[3])) + lam_init
    bd = hs.shape[0]
    past_len = page_table.shape[1] * PAGE_SIZE
    qp, kp, vp, zp = diff_project(hp, w_in, jnp.arange(hp.shape[1]))
    qs, ks, vs, zs = diff_project(hs, w_in, past_len + jnp.arange(hs.shape[1]))
    k_past = k_pool[j, page_table].reshape(bd, past_len, H_A, 2, D_A)
    v_past = v_pool[j, page_table].reshape(bd, past_len, H_A, 2 * D_A)
    o_p = diff_attn_prompt(qp, kp, vp, lam)
    o_s = diff_attn_sample(qs, ks, vs, k_past, v_past, lam)
    yp = diff_output(o_p, zp, subln_g, w_out, lam_init, hp.dtype)
    ys = diff_output(o_s, zs, subln_g, w_out, lam_init, hs.dtype)
    kp_rows = kp.reshape(kp.shape[0], kp.shape[1], H_A, 2 * D_A)
    ks_rows = ks.reshape(ks.shape[0], ks.shape[1], H_A, 2 * D_A)
    return yp, ys, kp_rows, vp, ks_rows, vs


def fox_project(h, w_in, b_f):
    b, l, _ = h.shape
    proj = h @ w_in
    q = proj[..., :E_B].reshape(b, l, H_B, D_B)
    k = proj[..., E_B:2 * E_B].reshape(b, l, H_B, D_B)
    v = proj[..., 2 * E_B:3 * E_B].reshape(b, l, H_B, D_B)
    z = proj[..., 3 * E_B:4 * E_B]
    logf = jax.nn.log_sigmoid((proj[..., 4 * E_B:] + b_f).astype(F32))
    return q, k, v, z, logf


def fox_attn_prompt(q, k, v, logf):
    b, l = q.shape[:2]
    nb = l // Q_BLOCK
    scale = D_B ** -0.5
    c = jnp.cumsum(logf, axis=1)
    qb = jnp.moveaxis(q.reshape(b, nb, Q_BLOCK, H_B, D_B), 1, 0)
    cb = jnp.moveaxis(c.reshape(b, nb, Q_BLOCK, H_B), 1, 0)
    ck = jnp.transpose(c, (0, 2, 1))[:, :, None, :]
    kf, vf = k.astype(F32), v.astype(F32)
    kpos = jnp.arange(l)

    def block(args):
        qi, ci, i = args
        s = jnp.einsum('bqhd,bkhd->bhqk', qi.astype(F32), kf) * scale
        s = s + jnp.transpose(ci, (0, 2, 1))[..., None] - ck
        qpos = i * Q_BLOCK + jnp.arange(Q_BLOCK)
        s = jnp.where(kpos[None, :] <= qpos[:, None], s, NEG_INF)
        a = jax.nn.softmax(s, axis=-1)
        return jnp.einsum('bhqk,bkhd->bqhd', a, vf)

    o = lax.map(block, (qb, cb, jnp.arange(nb)))
    return jnp.moveaxis(o, 0, 1).reshape(b, l, H_B, D_B)


def fox_attn_sample(q, k_new, v_new, logf_new, k_past, v_past, logf_past):
    scale = D_B ** -0.5
    t = q.shape[1]
    p = k_past.shape[1]
    c_past = jnp.cumsum(logf_past.astype(F32), axis=1)
    c_new = c_past[:, -1:, :] + jnp.cumsum(logf_new, axis=1)
    cq = jnp.transpose(c_new, (0, 2, 1))[..., None]
    qf = q.astype(F32)
    s_past = jnp.einsum('bqhd,bkhd->bhqk', qf, k_past.astype(F32)) * scale
    s_past = s_past + cq - jnp.transpose(c_past, (0, 2, 1))[:, :, None, :]
    s_new = jnp.einsum('bqhd,bkhd->bhqk', qf, k_new.astype(F32)) * scale
    s_new = s_new + cq - jnp.transpose(c_new, (0, 2, 1))[:, :, None, :]
    s_new = jnp.where(jnp.tril(jnp.ones((t, t), bool)), s_new, NEG_INF)
    a = jax.nn.softmax(jnp.concatenate([s_past, s_new], axis=-1), axis=-1)
    return (jnp.einsum('bhqk,bkhd->bqhd', a[..., :p], v_past.astype(F32))
            + jnp.einsum('bhqk,bkhd->bqhd', a[..., p:], v_new.astype(F32)))


def fox_output(o, z, w_out, dtype):
    b, l = o.shape[:2]
    o = o.reshape(b, l, E_B).astype(dtype) * jax.nn.silu(z)
    return (o @ w_out).astype(dtype)


def fox_layer(hp, hs, k_pool, v_pool, logf_pool, j, page_table, w_in, b_f, w_out):
    bd = hs.shape[0]
    past_len = page_table.shape[1] * PAGE_SIZE
    qp, kp, vp, zp, lfp = fox_project(hp, w_in, b_f)
    qs, ks, vs, zs, lfs = fox_project(hs, w_in, b_f)
    k_past = k_pool[j, page_table].reshape(bd, past_len, H_B, D_B)
    v_past = v_pool[j, page_table].reshape(bd, past_len, H_B, D_B)
    lf_past = logf_pool[j, page_table].reshape(bd, past_len, H_B)
    o_p = fox_attn_prompt(qp, kp, vp, lfp)
    o_s = fox_attn_sample(qs, ks, vs, lfs, k_past, v_past, lf_past)
    yp = fox_output(o_p, zp, w_out, hp.dtype)
    ys = fox_output(o_s, zs, w_out, hs.dtype)
    return yp, ys, kp, vp, lfp, ks, vs, lfs


def ssm_discretize(a_re, a_im, log_step, b_re, b_im):
    a_re = jnp.minimum(a_re.astype(F32), -1e-4)
    a_im = a_im.astype(F32)
    dt = jnp.exp(log_step.astype(F32))[:, None]
    mag = jnp.exp(a_re * dt)
    ang = a_im * dt
    lb_re, lb_im = mag * jnp.cos(ang), mag * jnp.sin(ang)
    den = a_re * a_re + a_im * a_im
    num_re, num_im = lb_re - 1.0, lb_im
    f_re = (num_re * a_re + num_im * a_im) / den
    f_im = (num_im * a_re - num_re * a_im) / den
    br, bi = b_re.astype(F32), b_im.astype(F32)
    bb_re = f_re[..., None] * br - f_im[..., None] * bi
    bb_im = f_re[..., None] * bi + f_im[..., None] * br
    return lb_re, lb_im, bb_re, bb_im


def ssm_combine(e1, e2):
    a1r, a1i, b1r, b1i = e1
    a2r, a2i, b2r, b2i = e2
    return (a2r * a1r - a2i * a1i, a2r * a1i + a2i * a1r,
            a2r * b1r - a2i * b1i + b2r, a2r * b1i + a2i * b1r + b2i)


def ssm_layer(h, h0_re, h0_im, disc, c_re, c_im, d, w_in, w_glu, w_out):
    b, l, _ = h.shape
    lb_re, lb_im, bb_re, bb_im = disc
    cr, ci, dd = c_re.astype(F32), c_im.astype(F32), d.astype(F32)
    proj = h @ w_in
    u = proj[..., :E_C].astype(F32).reshape(b, l, G_C, CH_C)
    z = proj[..., E_C:]

    def one_sequence(args):
        ui, hr0, hi0 = args
        bu_re = jnp.einsum('lgc,gpc->lgp', ui, bb_re)
        bu_im = jnp.einsum('lgc,gpc->lgp', ui, bb_im)
        bu_re = bu_re.at[0].add(lb_re * hr0 - lb_im * hi0)
        bu_im = bu_im.at[0].add(lb_re * hi0 + lb_im * hr0)
        a_re = jnp.broadcast_to(lb_re, bu_re.shape)
        a_im = jnp.broadcast_to(lb_im, bu_im.shape)
        _, _, hs_re, hs_im = lax.associative_scan(ssm_combine, (a_re, a_im, bu_re, bu_im), axis=0)
        y = (jnp.einsum('lgp,gcp->lgc', hs_re, cr) - jnp.einsum('lgp,gcp->lgc', hs_im, ci)
             + dd * ui)
        return y, hs_re[-1], hs_im[-1]

    y, h_re, h_im = lax.map(one_sequence, (u, h0_re.astype(F32), h0_im.astype(F32)))
    g = jax.nn.gelu(y.reshape(b, l, E_C))
    wg = w_glu.astype(F32)
    glu = (g @ wg[:, :E_C]) * jax.nn.sigmoid(g @ wg[:, E_C:])
    out = (glu.astype(h.dtype) * jax.nn.silu(z)) @ w_out
    return out.astype(h.dtype), h_re, h_im


def setup_inputs(seed: int = 0) -> dict:
    key = jax.random.key(seed)
    ks = jax.random.split(key, 32)
    n_pages = PAST_LEN // PAGE_SIZE
    n_used = DEC_BATCH * n_pages
    n_pool = n_used + n_used // 4 + 1

    def nrm(k, shape, scale=1.0):
        return jax.random.normal(k, shape, F32) * scale

    x_prompt = nrm(ks[0], (BATCH, SEQ, D_MODEL))
    x_sample = nrm(ks[1], (DEC_BATCH, DEC_SEQ, D_MODEL))
    cache_diff_k = nrm(ks[2], (N_A, n_pool, PAGE_SIZE, H_A, 2 * D_A))
    cache_diff_v = nrm(ks[3], (N_A, n_pool, PAGE_SIZE, H_A, 2 * D_A))
    cache_fox_k = nrm(ks[4], (N_B, n_pool, PAGE_SIZE, H_B, D_B))
    cache_fox_v = nrm(ks[5], (N_B, n_pool, PAGE_SIZE, H_B, D_B))
    cache_fox_logf = jax.nn.log_sigmoid(3.0 + nrm(ks[6], (N_B, n_pool, PAGE_SIZE, H_B)))
    state_ssm_re = nrm(ks[7], (N_C, DEC_BATCH, G_C, P_C), 0.1)
    state_ssm_im = nrm(ks[8], (N_C, DEC_BATCH, G_C, P_C), 0.1)
    page_table = jax.random.permutation(ks[9], n_pool)[:n_used].reshape(DEC_BATCH, n_pages).astype(jnp.int32)
    norm_g = 1.0 + nrm(ks[10], (DEPTH, D_MODEL), 0.01)
    final_norm_g = 1.0 + nrm(ks[11], (D_MODEL,), 0.01)
    diff_w_in = nrm(ks[12], (N_A, D_MODEL, 4 * E_A), D_MODEL ** -0.5)
    diff_lambda = nrm(ks[13], (N_A, 4, D_A), 0.1)
    diff_subln_g = 1.0 + nrm(ks[14], (N_A, 2 * D_A), 0.01)
    diff_w_out = nrm(ks[15], (N_A, E_A, D_MODEL), E_A ** -0.5)
    fox_w_in = nrm(ks[16], (N_B, D_MODEL, 4 * E_B + H_B), D_MODEL ** -0.5)
    fox_b_f = jnp.linspace(1.0, 5.0, H_B, dtype=F32)[None, :] + nrm(ks[17], (N_B, H_B), 0.01)
    fox_w_out = nrm(ks[18], (N_B, E_B, D_MODEL), E_B ** -0.5)
    ssm_w_in = nrm(ks[19], (N_C, D_MODEL, 2 * E_C), D_MODEL ** -0.5)
    ssm_a_re = -0.5 + nrm(ks[20], (N_C, G_C, P_C), 0.01)
    ssm_a_im = math.pi * jnp.arange(P_C, dtype=F32) + nrm(ks[21], (N_C, G_C, P_C), 0.01)
    ssm_log_step = jax.random.uniform(ks[22], (N_C, G_C), F32, math.log(1e-3), math.log(1e-1))
    ssm_b_re = nrm(ks[23], (N_C, G_C, P_C, CH_C), (2 * CH_C) ** -0.5)
    ssm_b_im = nrm(ks[24], (N_C, G_C, P_C, CH_C), (2 * CH_C) ** -0.5)
    ssm_c_re = nrm(ks[25], (N_C, G_C, CH_C, P_C), (2 * P_C) ** -0.5)
    ssm_c_im = nrm(ks[26], (N_C, G_C, CH_C, P_C), (2 * P_C) ** -0.5)
    ssm_d = nrm(ks[27], (N_C, G_C, CH_C))
    ssm_w_glu = nrm(ks[28], (N_C, E_C, 2 * E_C), E_C ** -0.5)
    ssm_w_out = nrm(ks[29], (N_C, E_C, D_MODEL), E_C ** -0.5)
    return {'x_prompt': x_prompt, 'x_sample': x_sample,
            'cache_diff_k': cache_diff_k, 'cache_diff_v': cache_diff_v,
            'cache_fox_k': cache_fox_k, 'cache_fox_v': cache_fox_v, 'cache_fox_logf': cache_fox_logf,
            'state_ssm_re': state_ssm_re, 'state_ssm_im': state_ssm_im,
            'page_table': page_table,
            'norm_g': norm_g, 'final_norm_g': final_norm_g,
            'diff_w_in': diff_w_in, 'diff_lambda': diff_lambda, 'diff_subln_g': diff_subln_g, 'diff_w_out': diff_w_out,
            'fox_w_in': fox_w_in, 'fox_b_f': fox_b_f, 'fox_w_out': fox_w_out,
            'ssm_w_in': ssm_w_in, 'ssm_a_re': ssm_a_re, 'ssm_a_im': ssm_a_im, 'ssm_log_step': ssm_log_step,
            'ssm_b_re': ssm_b_re, 'ssm_b_im': ssm_b_im, 'ssm_c_re': ssm_c_re, 'ssm_c_im': ssm_c_im,
            'ssm_d': ssm_d, 'ssm_w_glu': ssm_w_glu, 'ssm_w_out': ssm_w_out}


def reference(x_prompt, x_sample, cache_diff_k, cache_diff_v, cache_fox_k, cache_fox_v, cache_fox_logf,
              state_ssm_re, state_ssm_im, page_table, norm_g, final_norm_g,
              diff_w_in, diff_lambda, diff_subln_g, diff_w_out, fox_w_in, fox_b_f, fox_w_out,
              ssm_w_in, ssm_a_re, ssm_a_im, ssm_log_step, ssm_b_re, ssm_b_im, ssm_c_re, ssm_c_im,
              ssm_d, ssm_w_glu, ssm_w_out):
    xp, xs = x_prompt, x_sample
    dkp, dvp, dks, dvs = [], [], [], []
    fkp, fvp, flp, fks, fvs, fls = [], [], [], [], [], []
    srp, sip, srs, sis = [], [], [], []
    for i in range(DEPTH):
        kind, j = i % N_MIXERS, i // N_MIXERS
        hp = rms_norm(xp, norm_g[i])
        hs = rms_norm(xs, norm_g[i])
        if kind == 0:
            yp, ys, kp, vp, ks_, vs_ = diff_layer(hp, hs, cache_diff_k, cache_diff_v, j, page_table,
                                                 diff_w_in[j], diff_lambda[j], diff_subln_g[j],
                                                 diff_w_out[j], diff_lambda_init(i))
            dkp.append(kp); dvp.append(vp); dks.append(ks_); dvs.append(vs_)
        elif kind == 1:
            yp, ys, kp, vp, lfp, ks_, vs_, lfs = fox_layer(hp, hs, cache_fox_k, cache_fox_v, cache_fox_logf,
                                                         j, page_table, fox_w_in[j], fox_b_f[j], fox_w_out[j])
            fkp.append(kp); fvp.append(vp); flp.append(lfp)
            fks.append(ks_); fvs.append(vs_); fls.append(lfs)
        else:
            disc = ssm_discretize(ssm_a_re[j], ssm_a_im[j], ssm_log_step[j], ssm_b_re[j], ssm_b_im[j])
            zeros = jnp.zeros((hp.shape[0], G_C, P_C), F32)
            yp, hrp, hip = ssm_layer(hp, zeros, zeros, disc, ssm_c_re[j], ssm_c_im[j], ssm_d[j],
                                     ssm_w_in[j], ssm_w_glu[j], ssm_w_out[j])
            ys, hrs, his = ssm_layer(hs, state_ssm_re[j], state_ssm_im[j], disc, ssm_c_re[j], ssm_c_im[j],
                                     ssm_d[j], ssm_w_in[j], ssm_w_glu[j], ssm_w_out[j])
            srp.append(hrp); sip.append(hip); srs.append(hrs); sis.append(his)
        xp = xp + yp
        xs = xs + ys
    y_prompt = rms_norm(xp, final_norm_g)
    y_sample = rms_norm(xs, final_norm_g)
    new_diff_k_prompt = jnp.stack(dkp)
    new_diff_v_prompt = jnp.stack(dvp)
    new_diff_k_sample = jnp.stack(dks)
    new_diff_v_sample = jnp.stack(dvs)
    new_fox_k_prompt = jnp.stack(fkp)
    new_fox_v_prompt = jnp.stack(fvp)
    new_fox_logf_prompt = jnp.stack(flp)
    new_fox_k_sample = jnp.stack(fks)
    new_fox_v_sample = jnp.stack(fvs)
    new_fox_logf_sample = jnp.stack(fls)
    new_ssm_re_prompt = jnp.stack(srp)
    new_ssm_im_prompt = jnp.stack(sip)
    new_ssm_re_sample = jnp.stack(srs)
    new_ssm_im_sample = jnp.stack(sis)
    return (y_prompt, y_sample,
            new_diff_k_prompt, new_diff_v_prompt, new_diff_k_sample, new_diff_v_sample,
            new_fox_k_prompt, new_fox_v_prompt, new_fox_logf_prompt,
            new_fox_k_sample, new_fox_v_sample, new_fox_logf_sample,
            new_ssm_re_prompt, new_ssm_im_prompt, new_ssm_re_sample, new_ssm_im_sample)
```

```python
import functools
import math

import jax
import jax.numpy as jnp
from jax import lax
from jax.experimental import pallas as pl
from jax.experimental.pallas import tpu as pltpu

F32 = jnp.float32
BF16 = jnp.bfloat16
HIGHEST = lax.Precision.HIGHEST

LANES = 128
D_MODEL = 1024
PAGE_SIZE = 128
N_MIXERS = 3
D_HEAD = 64
SSM_CH = 16
SSM_P = 64
SSM_CHUNK = 16
NORM_EPS = 1e-6
SUBLN_EPS = 1e-5
NEG_INF = -1e30
ROPE_THETA = 10000.0
QK_SCALE = D_HEAD ** -0.5
VMEM_LIMIT = 56 * 1024 * 1024

NT_DIMS = (((1,), (1,)), ((), ()))
TN_DIMS = (((0,), (0,)), ((), ()))


def _params(*sem):
    return pltpu.CompilerParams(dimension_semantics=sem, vmem_limit_bytes=VMEM_LIMIT)


def _diff_lambda_init(layer_idx):
    return 0.8 - 0.6 * math.exp(-0.3 * layer_idx)


def _rms(x, g, eps):
    return x * lax.rsqrt(jnp.mean(x * x, axis=-1, keepdims=True) + eps) * g


def _silu(z):
    return z * jax.nn.sigmoid(z)


def _rope(a, cos, sin_signed, first_half):
    swapped = jnp.where(first_half, pltpu.roll(a, 96, 1), pltpu.roll(a, 32, 1))
    return a * cos + swapped * sin_signed


def _proj_body(*refs, mode):
    e = D_MODEL
    if mode == "diff":
        x_ref, g_ref, w_ref, cos_ref, sin_ref, qb_ref, k_ref, kb_ref, v_ref, vb_ref, z_ref = refs
    elif mode == "fox":
        x_ref, g_ref, w_ref, bf_ref, qb_ref, k_ref, kb_ref, v_ref, vb_ref, z_ref, lf_ref = refs
    else:
        x_ref, g_ref, w_ref, u_ref, z_ref = refs
    xn = _rms(x_ref[...], g_ref[...], NORM_EPS).astype(BF16)

    def proj(lo, width=e):
        return jnp.dot(xn, w_ref[:, lo:lo + width], preferred_element_type=F32)

    if mode == "ssm":
        u_ref[...] = proj(0)
        z_ref[...] = proj(e)
        return
    q = proj(0)
    k = proj(e)
    if mode == "diff":
        cos = cos_ref[...]
        sin = sin_ref[...]
        lane = lax.broadcasted_iota(jnp.int32, (1, LANES), 1)
        first_half = (lane % D_HEAD) < (D_HEAD // 2)
        for h in range(e // LANES):
            sl = slice(h * LANES, (h + 1) * LANES)
            qb_ref[:, sl] = (_rope(q[:, sl], cos, sin, first_half) * QK_SCALE).astype(BF16)
            kr = _rope(k[:, sl], cos, sin, first_half)
            k_ref[:, sl] = kr
            kb_ref[:, sl] = kr.astype(BF16)
    else:
        qb_ref[...] = (q * QK_SCALE).astype(BF16)
        k_ref[...] = k
        kb_ref[...] = k.astype(BF16)
    v = proj(2 * e)
    v_ref[...] = v
    vb_ref[...] = v.astype(BF16)
    z_ref[...] = proj(3 * e)
    if mode == "fox":
        n_heads = lf_ref.shape[-1]
        f = proj(4 * e, LANES)[:, :n_heads] + bf_ref[...]
        lf_ref[...] = jax.nn.log_sigmoid(f)


def _proj(x, g, w, mode, extras, tm):
    t, d = x.shape
    n = w.shape[1]
    row = lambda i: (i, 0)
    fixed = lambda i: (0, 0)
    in_specs = [pl.BlockSpec((tm, d), row), pl.BlockSpec((1, d), fixed), pl.BlockSpec((d, n), fixed)]
    f32_out = jax.ShapeDtypeStruct((t, d), F32)
    bf_out = jax.ShapeDtypeStruct((t, d), BF16)
    blk = pl.BlockSpec((tm, d), row)
    if mode == "diff":
        cos, sin = extras
        nblk = cos.shape[0] // tm
        tab = pl.BlockSpec((tm, LANES), lambda i: (i % nblk, 0))
        in_specs += [tab, tab]
        out_shape = [bf_out, f32_out, bf_out, f32_out, bf_out, f32_out]
        out_specs = [blk] * 6
    elif mode == "fox":
        (b_f,) = extras
        nh = b_f.shape[1]
        in_specs += [pl.BlockSpec((1, nh), fixed)]
        out_shape = [bf_out, f32_out, bf_out, f32_out, bf_out, f32_out, jax.ShapeDtypeStruct((t, nh), F32)]
        out_specs = [blk] * 6 + [pl.BlockSpec((tm, nh), row)]
    else:
        out_shape = [f32_out, f32_out]
        out_specs = [blk] * 2
    return pl.pallas_call(
        functools.partial(_proj_body, mode=mode),
        grid=(t // tm,),
        in_specs=in_specs,
        out_specs=out_specs,
        out_shape=out_shape,
        compiler_params=_params("parallel"),
        name=f"proj_{mode}",
    )(x, g.reshape(1, d), w, *extras)


def _rope_tables(pos):
    half = D_HEAD // 2
    inv = ROPE_THETA ** (-jnp.arange(half, dtype=F32) / half)
    ang = pos.astype(F32)[:, None] * inv[None, :]
    cos, sin = jnp.cos(ang), jnp.sin(ang)
    cos = jnp.concatenate([cos, cos, cos, cos], axis=1)
    sin = jnp.concatenate([-sin, sin, -sin, sin], axis=1)
    return cos, sin


def _attn_body(qi_ref, ki_ref, *refs, kind, lam_init, tq):
    if kind == "diff":
        q_ref, k_ref, v_ref, lam_ref, o_ref, m_sc, l_sc, acc_sc = refs
    else:
        q_ref, k_ref, v_ref, cq_ref, ck_ref, o_ref, m_sc, l_sc, acc_sc, cq_sc = refs
    h = pl.program_id(1)
    n = pl.program_id(2)
    qi = qi_ref[n]
    ki = ki_ref[n]
    lane = lax.broadcasted_iota(jnp.int32, (1, LANES), 1)
    low = lane < D_HEAD

    @pl.when(ki == 0)
    def _init():
        m_sc[...] = jnp.full(m_sc.shape, NEG_INF, F32)
        l_sc[...] = jnp.zeros(l_sc.shape, F32)
        acc_sc[...] = jnp.zeros(acc_sc.shape, F32)
        if kind == "fox":
            c = cq_ref[...]
            col = lax.broadcasted_iota(jnp.int32, (1, c.shape[1]), 1)
            for s in range(2):
                cq_sc[s] = jnp.sum(jnp.where(col == 2 * h + s, c, 0.0), axis=1, keepdims=True)

    def step(masked):
        q = q_ref[...]
        k = k_ref[...]
        v = v_ref[...]
        for s in range(2):
            qs = jnp.where(low if s == 0 else jnp.logical_not(low), q, jnp.zeros_like(q))
            sc = lax.dot_general(qs, k, NT_DIMS, preferred_element_type=F32)
            if kind == "fox":
                sc = sc + (cq_sc[s] - ck_ref[pl.ds(2 * h + s, 1), :])
            if masked:
                r = lax.broadcasted_iota(jnp.int32, sc.shape, 0)
                c = lax.broadcasted_iota(jnp.int32, sc.shape, 1)
                sc = jnp.where(c <= r, sc, NEG_INF)
            m_prev = m_sc[s]
            m_new = jnp.maximum(m_prev, jnp.max(sc, axis=1, keepdims=True))
            alpha = jnp.exp(m_prev - m_new)
            p = jnp.exp(sc - m_new)
            l_sc[s] = alpha * l_sc[s] + jnp.sum(p, axis=1, keepdims=True)
            acc_sc[s] = alpha * acc_sc[s] + jnp.dot(p.astype(BF16), v, preferred_element_type=F32)
            m_sc[s] = m_new

    @pl.when(ki < qi)
    def _full():
        step(False)

    @pl.when(ki == qi)
    def _diag():
        step(True)
        o0 = acc_sc[0] / l_sc[0]
        o1 = acc_sc[1] / l_sc[1]
        if kind == "diff":
            lv = lam_ref[...]
            lam = (jnp.exp(jnp.sum(lv[0:1] * lv[1:2], axis=1, keepdims=True))
                   - jnp.exp(jnp.sum(lv[2:3] * lv[3:4], axis=1, keepdims=True)) + lam_init)
            o_ref[...] = o0 - lam * o1
        else:
            o_ref[...] = jnp.where(low, o0, o1)


def _attn_prompt(qb, kb, vb, kind, extras, lam_init, tq):
    b, l, e = qb.shape
    nq = l // tq
    pairs = [(i, j) for i in range(nq) for j in range(i + 1)]
    qi = jnp.asarray([p[0] for p in pairs], jnp.int32)
    ki = jnp.asarray([p[1] for p in pairs], jnp.int32)
    q_spec = pl.BlockSpec((None, tq, LANES), lambda bb, h, n, qi, ki: (bb, qi[n], h))
    kv_spec = pl.BlockSpec((None, tq, LANES), lambda bb, h, n, qi, ki: (bb, ki[n], h))
    in_specs = [q_spec, kv_spec, kv_spec]
    scratch = [pltpu.VMEM((2, tq, 1), F32), pltpu.VMEM((2, tq, 1), F32), pltpu.VMEM((2, tq, LANES), F32)]
    if kind == "diff":
        (lam,) = extras
        in_specs += [pl.BlockSpec(lam.shape, lambda bb, h, n, qi, ki: (0, 0))]
    else:
        c_cols, c_rows = extras
        nh = c_cols.shape[-1]
        in_specs += [pl.BlockSpec((None, tq, nh), lambda bb, h, n, qi, ki: (bb, qi[n], 0)),
                     pl.BlockSpec((None, nh, tq), lambda bb, h, n, qi, ki: (bb, 0, ki[n]))]
        scratch += [pltpu.VMEM((2, tq, 1), F32)]
    return pl.pallas_call(
        functools.partial(_attn_body, kind=kind, lam_init=lam_init, tq=tq),
        grid_spec=pltpu.PrefetchScalarGridSpec(
            num_scalar_prefetch=2,
            grid=(b, e // LANES, len(pairs)),
            in_specs=in_specs,
            out_specs=q_spec,
            scratch_shapes=scratch),
        out_shape=jax.ShapeDtypeStruct((b, l, e), F32),
        compiler_params=_params("parallel", "parallel", "arbitrary"),
        name=f"attn_{kind}",
    )(qi, ki, qb, kb, vb, *extras)


def _cumsum_body(lf_ref, cols_ref, rows_ref, carry_sc):
    @pl.when(pl.program_id(1) == 0)
    def _():
        carry_sc[...] = jnp.zeros(carry_sc.shape, F32)

    lf = lf_ref[...]
    t = lf.shape[0]
    r = lax.broadcasted_iota(jnp.int32, (t, t), 0)
    c = lax.broadcasted_iota(jnp.int32, (t, t), 1)
    tri = (c <= r).astype(F32)
    cs = jnp.dot(tri, lf, preferred_element_type=F32, precision=HIGHEST) + carry_sc[...]
    cols_ref[...] = cs
    rows_ref[...] = cs.T
    carry_sc[...] = cs[t - 1:t, :]


def _cumsum(logf, tm):
    b, l, nh = logf.shape
    return pl.pallas_call(
        _cumsum_body,
        grid=(b, l // tm),
        in_specs=[pl.BlockSpec((None, tm, nh), lambda bb, i: (bb, i, 0))],
        out_specs=[pl.BlockSpec((None, tm, nh), lambda bb, i: (bb, i, 0)),
                   pl.BlockSpec((None, nh, tm), lambda bb, i: (bb, 0, i))],
        out_shape=[jax.ShapeDtypeStruct((b, l, nh), F32), jax.ShapeDtypeStruct((b, nh, l), F32)],
        scratch_shapes=[pltpu.VMEM((1, nh), F32)],
        compiler_params=_params("parallel", "arbitrary"),
        name="fox_cumsum",
    )(logf)


def _oproj_body(*refs, mode, lam_init, final):
    refs = list(refs)
    a_ref, z_ref, x_ref = refs[:3]
    refs = refs[3:]
    if mode == "ssm":
        wg_ref = refs.pop(0)
    w_ref = refs.pop(0)
    if mode == "diff":
        sg_ref = refs.pop(0)
    if final:
        fg_ref = refs.pop(0)
    out_ref = refs.pop(0)
    e = D_MODEL
    z = z_ref[...]
    if mode == "diff":
        a = a_ref[...]
        sg = sg_ref[...] * (1.0 - lam_init)
        a = jnp.concatenate(
            [_rms(a[:, h * LANES:(h + 1) * LANES], 1.0, SUBLN_EPS) * sg for h in range(e // LANES)], axis=1)
    elif mode == "ssm":
        gl = jnp.dot(a_ref[...], wg_ref[...], preferred_element_type=F32)
        a = gl[:, :e] * jax.nn.sigmoid(gl[:, e:])
    else:
        a = a_ref[...]
    y = jnp.dot((a * _silu(z)).astype(BF16), w_ref[...], preferred_element_type=F32)
    xn = x_ref[...] + y
    if final:
        out_ref[...] = _rms(xn, fg_ref[...], NORM_EPS)
    else:
        out_ref[...] = xn


def _oproj(a, z, x, w, mode, tm, wg=None, subln_g=None, lam_init=0.0, final_g=None):
    t, d = x.shape
    row = lambda i: (i, 0)
    fixed = lambda i: (0, 0)
    blk = pl.BlockSpec((tm, d), row)
    args = [a, z, x]
    in_specs = [blk, blk, blk]
    if mode == "ssm":
        args.append(wg)
        in_specs.append(pl.BlockSpec(wg.shape, fixed))
    args.append(w)
    in_specs.append(pl.BlockSpec(w.shape, fixed))
    if mode == "diff":
        args.append(subln_g.reshape(1, LANES))
        in_specs.append(pl.BlockSpec((1, LANES), fixed))
    if final_g is not None:
        args.append(final_g.reshape(1, d))
        in_specs.append(pl.BlockSpec((1, d), fixed))
    return pl.pallas_call(
        functools.partial(_oproj_body, mode=mode, lam_init=lam_init, final=final_g is not None),
        grid=(t // tm,),
        in_specs=in_specs,
        out_specs=blk,
        out_shape=jax.ShapeDtypeStruct((t, d), F32),
        compiler_params=_params("parallel"),
        name=f"oproj_{mode}",
    )(*args)


def _decode_body(pt_ref, *refs, kind, lam_init, n_pages):
    del pt_ref
    refs = list(refs)
    q_ref, kn_ref, vn_ref = refs[:3]
    refs = refs[3:]
    if kind == "diff":
        lam_ref = refs.pop(0)
    else:
        lfn_ref = refs.pop(0)
    k_refs = refs[:n_pages]
    v_refs = refs[n_pages:2 * n_pages]
    refs = refs[2 * n_pages:]
    if kind == "fox":
        lf_refs = refs[:n_pages]
        refs = refs[n_pages:]
    o_ref = refs[0]
    e = D_MODEL
    n_rows = e // D_HEAD
    b = pl.program_id(0)
    row = lax.broadcasted_iota(jnp.int32, (n_rows, e), 0)
    lane = lax.broadcasted_iota(jnp.int32, (n_rows, e), 1)
    own = (lane // D_HEAD) == row
    q = jnp.where(own, q_ref[...], 0.0)
    qb = q.astype(BF16)
    s = jnp.concatenate(
        [lax.dot_general(qb, k_refs[p][...].astype(BF16), NT_DIMS, preferred_element_type=F32)
         for p in range(n_pages)], axis=1)
    s_new = jnp.sum(q * kn_ref[...], axis=1, keepdims=True)
    if kind == "fox":
        page = lf_refs[0].shape[0]
        lft = jnp.concatenate([lf_refs[p][...].T for p in range(n_pages)], axis=0)
        r = lax.broadcasted_iota(jnp.int32, (page, page), 0)
        c = lax.broadcasted_iota(jnp.int32, (page, page), 1)
        within = jnp.dot(lft, (r <= c).astype(F32), preferred_element_type=F32, precision=HIGHEST)
        offs = jnp.zeros((n_rows, 1), F32)
        c_past = []
        for p in range(n_pages):
            w = within[p * n_rows:(p + 1) * n_rows]
            c_past.append(w + offs)
            offs = offs + w[:, page - 1:page]
        c_past = jnp.concatenate(c_past, axis=1)
        lfn = lfn_ref[...]
        bcol = lax.broadcasted_iota(jnp.int32, lfn.shape, 1)
        c_new = offs + jnp.sum(jnp.where(bcol == b, lfn, 0.0), axis=1, keepdims=True)
        s = s + (c_new - c_past)
    m = jnp.maximum(jnp.max(s, axis=1, keepdims=True), s_new)
    p_past = jnp.exp(s - m)
    p_new = jnp.exp(s_new - m)
    l = jnp.sum(p_past, axis=1, keepdims=True) + p_new
    pb = p_past.astype(BF16)
    page = k_refs[0].shape[0]
    acc = p_new * vn_ref[...]
    for p in range(n_pages):
        acc = acc + jnp.dot(pb[:, p * page:(p + 1) * page], v_refs[p][...].astype(BF16),
                            preferred_element_type=F32)
    o = acc / l
    if kind == "diff":
        lv = lam_ref[...]
        lam = (jnp.exp(jnp.sum(lv[0:1] * lv[1:2], axis=1, keepdims=True))
               - jnp.exp(jnp.sum(lv[2:3] * lv[3:4], axis=1, keepdims=True)) + lam_init)
        head = (lane // LANES) == (row // 2)
        o0 = jnp.sum(jnp.where(head & (row % 2 == 0), o, 0.0), axis=0, keepdims=True)
        o1 = jnp.sum(jnp.where(head & (row % 2 == 1), o, 0.0), axis=0, keepdims=True)
        o_ref[...] = o0 - lam * o1
    else:
        o_ref[...] = jnp.sum(jnp.where(own, o, 0.0), axis=0, keepdims=True)


def _decode_attn(q, k_new, v_new, k_pool, v_pool, page_table, kind, extras, lam_init, layer):
    bd, e = q.shape
    n_pages = page_table.shape[1]
    vec = pl.BlockSpec((None, 1, e), lambda b, pt: (b, 0, 0))
    in_specs = [vec, vec, vec]
    args = [q.reshape(bd, 1, e), k_new.reshape(bd, 1, e), v_new.reshape(bd, 1, e)]
    if kind == "diff":
        (lam,) = extras
        args.append(lam)
        in_specs.append(pl.BlockSpec(lam.shape, lambda b, pt: (0, 0)))
    else:
        lf_new_t, lf_pool = extras
        args.append(lf_new_t)
        in_specs.append(pl.BlockSpec(lf_new_t.shape, lambda b, pt: (0, 0)))

    def page_spec(p, width):
        return pl.BlockSpec((None, None, PAGE_SIZE, width), lambda b, pt: (layer, pt[b, p], 0, 0))

    for pool in (k_pool, v_pool):
        args += [pool] * n_pages
        in_specs += [page_spec(p, e) for p in range(n_pages)]
    if kind == "fox":
        args += [lf_pool] * n_pages
        in_specs += [page_spec(p, lf_pool.shape[-1]) for p in range(n_pages)]
    out = pl.pallas_call(
        functools.partial(_decode_body, kind=kind, lam_init=lam_init, n_pages=n_pages),
        grid_spec=pltpu.PrefetchScalarGridSpec(
            num_scalar_prefetch=1,
            grid=(bd,),
            in_specs=in_specs,
            out_specs=vec),
        out_shape=jax.ShapeDtypeStruct((bd, 1, e), F32),
        compiler_params=_params("parallel"),
        name=f"decode_{kind}",
    )(page_table, *args)
    return out.reshape(bd, e)


def _ssm_disc_body(are_ref, aim_ref, dt_ref, pr_ref, pi_ref, fre_ref, fim_ref):
    a_re = jnp.minimum(are_ref[...], -1e-4)
    a_im = aim_ref[...]
    dt = dt_ref[...]
    mag = jnp.exp(a_re * dt)
    ang = a_im * dt
    lb_re = mag * jnp.cos(ang)
    lb_im = mag * jnp.sin(ang)
    den = a_re * a_re + a_im * a_im
    num_re = lb_re - 1.0
    num_im = lb_im
    fre_ref[...] = (num_re * a_re + num_im * a_im) / den
    fim_ref[...] = (num_im * a_re - num_re * a_im) / den
    pr = jnp.ones_like(lb_re)
    pi = jnp.zeros_like(lb_im)
    for j in range(SSM_CHUNK + 1):
        pr_ref[j] = pr
        pi_ref[j] = pi
        pr, pi = pr * lb_re - pi * lb_im, pr * lb_im + pi * lb_re


def _twice(a):
    return jnp.concatenate([a, a], axis=-1).astype(F32)


def _ssm_discretize(a_re, a_im, log_step):
    g, p = a_re.shape
    dt = jnp.broadcast_to(jnp.exp(log_step.astype(F32))[:, None], (g, 2 * p))
    pw = jax.ShapeDtypeStruct((SSM_CHUNK + 1, g, 2 * p), F32)
    gp = jax.ShapeDtypeStruct((g, 2 * p), F32)
    return pl.pallas_call(_ssm_disc_body, out_shape=[pw, pw, gp, gp], name="ssm_discretize")(
        _twice(a_re), _twice(a_im), dt)


def _ssm_tables_body(pr_ref, pi_ref, fre_ref, fim_ref, br_ref, bi_ref, cr_ref, ci_ref, sel_ref,
                     t_ref, e_ref, fr_ref, fi_ref, bb_ref):
    n = SSM_CHUNK
    ch = SSM_CH
    low = lax.broadcasted_iota(jnp.int32, (1, 2 * SSM_P), 1) < SSM_P
    fre = fre_ref[...]
    fim = fim_ref[...]
    br = br_ref[...]
    bi = bi_ref[...]
    bbr = fre * br - fim * bi
    bbi = fre * bi + fim * br
    bb_ref[...] = jnp.where(low, bbr, bbi)
    cr = cr_ref[...]
    ci = ci_ref[...]
    x_rows = []
    for j in range(n):
        pr = pr_ref[n - 1 - j]
        pi = pi_ref[n - 1 - j]
        x_rows.append(bbr * jnp.where(low, pr, pi) + bbi * jnp.where(low, -pi, pr))
    x_all = jnp.concatenate(x_rows, axis=0)
    sel = sel_ref[...]
    e_ref[...] = jnp.dot(x_all, sel, preferred_element_type=F32, precision=HIGHEST)
    c_cat = jnp.where(low, cr, -ci)
    c_rep = jnp.concatenate([c_cat] * n, axis=0)
    k_rev = lax.dot_general(x_all, c_rep, NT_DIMS, preferred_element_type=F32, precision=HIGHEST)
    blk = lax.broadcasted_iota(jnp.int32, (n * ch, n * ch), 1) // ch
    toe = jnp.zeros((n * ch, n * ch), F32)
    for t in range(n):
        sh = ch * (n - 1 - t)
        shifted = k_rev if sh == 0 else jnp.concatenate([k_rev[sh:], jnp.zeros((sh, n * ch), F32)], axis=0)
        toe = jnp.where(blk == t, shifted, toe)
    t_ref[...] = toe
    f_rows = []
    for t in range(n):
        pr = pr_ref[t + 1]
        pi = pi_ref[t + 1]
        f_rows.append(cr * jnp.where(low, pr, -pi) + ci * jnp.where(low, -pi, -pr))
    f_all = jnp.dot(jnp.concatenate(f_rows, axis=0), sel, preferred_element_type=F32, precision=HIGHEST)
    p2 = 2 * SSM_P
    fr_ref[...] = f_all[:, :p2]
    fi_ref[...] = f_all[:, p2:]


def _ssm_tables(pw_re, pw_im, f_re, f_im, b_re, b_im, c_re, c_im):
    g, p2 = f_re.shape
    p = p2 // 2
    n, ch = SSM_CHUNK, SSM_CH
    nc = n * ch
    eye = jnp.eye(p, dtype=F32)
    zero = jnp.zeros((p, p), F32)
    sel = jnp.stack([
        jnp.block([[eye, zero, zero, zero], [zero, zero, eye, zero]]),
        jnp.block([[zero, eye, zero, zero], [zero, zero, zero, eye]])])
    gmap = lambda i: (i, 0, 0)
    pw_spec = pl.BlockSpec((n + 1, None, 1, p2), lambda i: (0, i, 0, 0))
    row_spec = pl.BlockSpec((None, 1, p2), gmap)
    mat_spec = pl.BlockSpec((None, ch, p2), gmap)
    out = lambda rows, cols: (jax.ShapeDtypeStruct((g, rows, cols), F32), pl.BlockSpec((None, rows, cols), gmap))
    outs = [out(nc, nc), out(nc, 4 * p), out(nc, p2), out(nc, p2), out(ch, p2)]
    return pl.pallas_call(
        _ssm_tables_body,
        grid=(g,),
        in_specs=[pw_spec, pw_spec, row_spec, row_spec, mat_spec, mat_spec, mat_spec, mat_spec,
                  pl.BlockSpec((None, p2, 4 * p), lambda i: (i % 2, 0, 0))],
        out_specs=[o[1] for o in outs],
        out_shape=[o[0] for o in outs],
        compiler_params=_params("parallel"),
        name="ssm_tables",
    )(pw_re.reshape(n + 1, g, 1, p2), pw_im.reshape(n + 1, g, 1, p2), f_re.reshape(g, 1, p2), f_im.reshape(g, 1, p2),
      _twice(jnp.swapaxes(b_re, 1, 2)), _twice(jnp.swapaxes(b_im, 1, 2)), _twice(c_re), _twice(c_im), sel)


def _ssm_local_body(u_ref, e_ref, sre_ref, sim_ref):
    s = (jnp.dot(u_ref[0].astype(BF16), e_ref[0].astype(BF16), preferred_element_type=F32)
         + jnp.dot(u_ref[1].astype(BF16), e_ref[1].astype(BF16), preferred_element_type=F32))
    half = s.shape[1] // 2
    sre_ref[...] = s[:, :half]
    sim_ref[...] = s[:, half:]


def _ssm_local(ug, e_tab):
    g, nchunk, nc = ug.shape
    p = SSM_P
    st = jax.ShapeDtypeStruct((nchunk, g * p), F32)
    return pl.pallas_call(
        _ssm_local_body,
        grid=(g // 2,),
        in_specs=[pl.BlockSpec((2, nchunk, nc), lambda i: (i, 0, 0)),
                  pl.BlockSpec((2, nc, 4 * p), lambda i: (i, 0, 0))],
        out_specs=[pl.BlockSpec((nchunk, 2 * p), lambda i: (0, i))] * 2,
        out_shape=[st, st],
        compiler_params=_params("parallel"),
        name="ssm_local",
    )(ug, e_tab)


def _ssm_scan_body(sre_ref, sim_ref, ar_ref, ai_ref, hre_ref, him_ref, fre_ref, fim_ref, *, per_seq):
    n_seq = sre_ref.shape[0] // per_seq
    ar = ar_ref[...]
    ai = ai_ref[...]
    zero = jnp.zeros_like(ar)

    def body(i, carry):
        new = []
        for q in range(n_seq):
            hr, hi = carry[2 * q], carry[2 * q + 1]
            row = q * per_seq + i
            hre_ref[pl.ds(row, 1), :] = hr
            him_ref[pl.ds(row, 1), :] = hi
            new.append(ar * hr - ai * hi + sre_ref[pl.ds(row, 1), :])
            new.append(ar * hi + ai * hr + sim_ref[pl.ds(row, 1), :])
        return tuple(new)

    fin = lax.fori_loop(0, per_seq, body, (zero,) * (2 * n_seq))
    for q in range(n_seq):
        fre_ref[pl.ds(q, 1), :] = fin[2 * q]
        fim_ref[pl.ds(q, 1), :] = fin[2 * q + 1]


def _ssm_scan(s_re, s_im, a_re, a_im, n_seq):
    rows, width = s_re.shape
    wblk = 1024
    full = jax.ShapeDtypeStruct((rows, width), F32)
    fin = jax.ShapeDtypeStruct((n_seq, width), F32)
    col = lambda i: (0, i)
    return pl.pallas_call(
        functools.partial(_ssm_scan_body, per_seq=rows // n_seq),
        grid=(width // wblk,),
        in_specs=[pl.BlockSpec((rows, wblk), col)] * 2 + [pl.BlockSpec((1, wblk), col)] * 2,
        out_specs=[pl.BlockSpec((rows, wblk), col)] * 2 + [pl.BlockSpec((n_seq, wblk), col)] * 2,
        out_shape=[full, full, fin, fin],
        compiler_params=_params("parallel"),
        name="ssm_scan",
    )(s_re, s_im, a_re, a_im)


def _ssm_out_body(u_ref, t_ref, fr_ref, fi_ref, d_ref, hre_ref, him_ref, g_ref):
    hre = hre_ref[...].astype(BF16)
    him = him_ref[...].astype(BF16)
    for a in range(2):
        u = u_ref[a]
        y = jnp.dot(u.astype(BF16), t_ref[a].astype(BF16), preferred_element_type=F32)
        y = y + lax.dot_general(hre, fr_ref[a].astype(BF16), NT_DIMS, preferred_element_type=F32)
        y = y + lax.dot_general(him, fi_ref[a].astype(BF16), NT_DIMS, preferred_element_type=F32)
        y = y + d_ref[a] * u
        g_ref[a] = jax.nn.gelu(y).astype(BF16)


def _ssm_out(ug, t_tab, fr_tab, fi_tab, d_rows, h_re, h_im):
    g, nchunk, nc = ug.shape
    p = SSM_P
    pair3 = lambda i: (i, 0, 0)
    return pl.pallas_call(
        _ssm_out_body,
        grid=(g // 2,),
        in_specs=[pl.BlockSpec((2, nchunk, nc), pair3),
                  pl.BlockSpec((2, nc, nc), pair3),
                  pl.BlockSpec((2, nc, 2 * p), pair3),
                  pl.BlockSpec((2, nc, 2 * p), pair3),
                  pl.BlockSpec((2, 1, nc), pair3),
                  pl.BlockSpec((nchunk, 2 * p), lambda i: (0, i)),
                  pl.BlockSpec((nchunk, 2 * p), lambda i: (0, i))],
        out_specs=pl.BlockSpec((2, nchunk, nc), pair3),
        out_shape=jax.ShapeDtypeStruct((g, nchunk, nc), BF16),
        compiler_params=_params("parallel"),
        name="ssm_out",
    )(ug, t_tab, fr_tab, fi_tab, d_rows, h_re, h_im)


def _ssm_step_body(u_ref, hr_ref, hi_ref, lr_ref, li_ref, wb_ref, cr_ref, ci_ref, d_ref,
                   g_ref, nr_ref, ni_ref):
    u = u_ref[...]
    bu = jnp.dot(u.astype(BF16), wb_ref[...].astype(BF16), preferred_element_type=F32)
    half = bu.shape[1] // 2
    lr, li, hr, hi = lr_ref[...], li_ref[...], hr_ref[...], hi_ref[...]
    nr = lr * hr - li * hi + bu[:, :half]
    ni = lr * hi + li * hr + bu[:, half:]
    nr_ref[...] = nr
    ni_ref[...] = ni
    y = (jnp.dot(nr.astype(BF16), cr_ref[...].astype(BF16), preferred_element_type=F32)
         - jnp.dot(ni.astype(BF16), ci_ref[...].astype(BF16), preferred_element_type=F32)
         + d_ref[...] * u)
    g_ref[...] = jax.nn.gelu(y).astype(BF16)


def _ssm_step(u, h_re, h_im, lb_re, lb_im, bb, c_re, c_im, d):
    bd, e = u.shape
    g, ch, p2 = bb.shape
    p = p2 // 2
    gs = LANES // ch
    ns = g // gs
    eye = jnp.eye(gs, dtype=F32)
    wb = jnp.einsum("sgcrp,gh->sgcrhp", bb.reshape(ns, gs, ch, 2, p), eye).reshape(ns, LANES, 2 * gs * p)
    crb = jnp.einsum("sgcp,gh->shpgc", c_re.reshape(ns, gs, ch, p), eye).reshape(ns, gs * p, LANES)
    cib = jnp.einsum("sgcp,gh->shpgc", c_im.reshape(ns, gs, ch, p), eye).reshape(ns, gs * p, LANES)
    st = jax.ShapeDtypeStruct((bd, g * p), F32)
    col = lambda i: (0, i)
    slab = lambda i: (i, 0, 0)
    return pl.pallas_call(
        _ssm_step_body,
        grid=(ns,),
        in_specs=[pl.BlockSpec((bd, LANES), col),
                  pl.BlockSpec((bd, gs * p), col), pl.BlockSpec((bd, gs * p), col),
                  pl.BlockSpec((1, gs * p), col), pl.BlockSpec((1, gs * p), col),
                  pl.BlockSpec((None, LANES, 2 * gs * p), slab),
                  pl.BlockSpec((None, gs * p, LANES), slab), pl.BlockSpec((None, gs * p, LANES), slab),
                  pl.BlockSpec((1, LANES), col)],
        out_specs=[pl.BlockSpec((bd, LANES), col),
                   pl.BlockSpec((bd, gs * p), col), pl.BlockSpec((bd, gs * p), col)],
        out_shape=[jax.ShapeDtypeStruct((bd, e), BF16), st, st],
        compiler_params=_params("parallel"),
        name="ssm_step",
    )(u, h_re, h_im, lb_re, lb_im, wb, crb, cib, d.reshape(1, g * ch))


def _row_tile(t):
    return min(t, 256)


def _diff_layer(xp, xs, k_pool, v_pool, j, page_table, g, w_in, lam_vec, subln_g, w_out, lam_init, final_g, seq):
    wb = w_in.astype(BF16)
    wo = w_out.astype(BF16)
    lam = lam_vec.astype(F32)
    b = xp.shape[0] // seq
    past = page_table.shape[1] * PAGE_SIZE
    outs = []
    for x, pos in ((xp, jnp.arange(seq)), (xs, jnp.full((xs.shape[0],), past))):
        cos, sin = _rope_tables(pos)
        outs.append(_proj(x, g, wb, "diff", (cos, sin), _row_tile(x.shape[0])))
    (qbp, kp, kbp, vp, vbp, zp), (qbs, ks, _, vs, _, zs) = outs
    e = xp.shape[1]
    r3 = lambda a: a.reshape(b, seq, e)
    op = _attn_prompt(r3(qbp), r3(kbp), r3(vbp), "diff", (lam,), lam_init, 512).reshape(b * seq, e)
    os_ = _decode_attn(qbs.astype(F32), ks, vs, k_pool, v_pool, page_table, "diff", (lam,), lam_init, j)
    xp = _oproj(op, zp, xp, wo, "diff", 512, subln_g=subln_g, lam_init=lam_init, final_g=final_g)
    xs = _oproj(os_, zs, xs, wo, "diff", _row_tile(xs.shape[0]), subln_g=subln_g, lam_init=lam_init,
                final_g=final_g)
    return xp, xs, kp, vp, ks, vs


def _fox_layer(xp, xs, k_pool, v_pool, lf_pool, j, page_table, g, w_in, b_f, w_out, seq):
    d, n = w_in.shape
    nh = b_f.shape[0]
    wb = jnp.pad(w_in, ((0, 0), (0, LANES - nh))).astype(BF16)
    wo = w_out.astype(BF16)
    bf = b_f.reshape(1, nh).astype(F32)
    b = xp.shape[0] // seq
    qbp, kp, kbp, vp, vbp, zp, lfp = _proj(xp, g, wb, "fox", (bf,), _row_tile(xp.shape[0]))
    qbs, ks, _, vs, _, zs, lfs = _proj(xs, g, wb, "fox", (bf,), _row_tile(xs.shape[0]))
    e = xp.shape[1]
    r3 = lambda a: a.reshape(b, seq, e)
    c_cols, c_rows = _cumsum(lfp.reshape(b, seq, nh), 256)
    op = _attn_prompt(r3(qbp), r3(kbp), r3(vbp), "fox", (c_cols, c_rows), 0.0, 512).reshape(b * seq, e)
    os_ = _decode_attn(qbs.astype(F32), ks, vs, k_pool, v_pool, page_table, "fox", (lfs.T, lf_pool), 0.0, j)
    xp = _oproj(op, zp, xp, wo, "fox", 512)
    xs = _oproj(os_, zs, xs, wo, "fox", _row_tile(xs.shape[0]))
    return xp, xs, kp, vp, lfp, ks, vs, lfs


def _ssm_layer(xp, xs, h0_re, h0_im, g, w_in, a_re, a_im, log_step, b_re, b_im, c_re, c_im, d, w_glu, w_out, seq):
    ng, p = a_re.shape
    ch, n = SSM_CH, SSM_CHUNK
    wb = w_in.astype(BF16)
    wg = w_glu.astype(BF16)
    wo = w_out.astype(BF16)
    b = xp.shape[0] // seq
    e = xp.shape[1]
    pw_re, pw_im, f_re, f_im = _ssm_discretize(a_re, a_im, log_step)
    t_tab, e_tab, fr_tab, fi_tab, bb = _ssm_tables(pw_re, pw_im, f_re, f_im, b_re, b_im, c_re, c_im)
    dd = d.astype(F32)
    up, zp = _proj(xp, g, wb, "ssm", (), _row_tile(xp.shape[0]))
    nchunk = b * seq // n
    ug = up.reshape(nchunk, n, ng, ch).transpose(2, 0, 1, 3).reshape(ng, nchunk, n * ch)
    s_re, s_im = _ssm_local(ug, e_tab)
    flat = lambda a: a[:, :p].reshape(1, ng * p)
    h_re, h_im, fin_re, fin_im = _ssm_scan(s_re, s_im, flat(pw_re[n]), flat(pw_im[n]), b)
    d_rows = jnp.tile(dd[:, None, :], (1, n, 1)).reshape(ng, 1, n * ch)
    gg = _ssm_out(ug, t_tab, fr_tab, fi_tab, d_rows, h_re, h_im)
    gp = gg.reshape(ng, nchunk, n, ch).transpose(1, 2, 0, 3).reshape(b * seq, e)
    xp = _oproj(gp, zp, xp, wo, "ssm", 512, wg=wg)
    us, zs = _proj(xs, g, wb, "ssm", (), _row_tile(xs.shape[0]))
    bd = xs.shape[0]
    gs, ns_re, ns_im = _ssm_step(us, h0_re.reshape(bd, ng * p).astype(F32), h0_im.reshape(bd, ng * p).astype(F32),
                                 flat(pw_re[1]), flat(pw_im[1]), bb, c_re.astype(F32), c_im.astype(F32), dd)
    xs = _oproj(gs, zs, xs, wo, "ssm", _row_tile(bd), wg=wg)
    return xp, xs, fin_re.reshape(b, ng, p), fin_im.reshape(b, ng, p), ns_re.reshape(bd, ng, p), ns_im.reshape(bd, ng, p)


def kernel(x_prompt, x_sample, cache_diff_k, cache_diff_v, cache_fox_k, cache_fox_v, cache_fox_logf, state_ssm_re, state_ssm_im, page_table, norm_g, final_norm_g, diff_w_in, diff_lambda, diff_subln_g, diff_w_out, fox_w_in, fox_b_f, fox_w_out, ssm_w_in, ssm_a_re, ssm_a_im, ssm_log_step, ssm_b_re, ssm_b_im, ssm_c_re, ssm_c_im, ssm_d, ssm_w_glu, ssm_w_out):
    b, seq, dm = x_prompt.shape
    bd = x_sample.shape[0]
    depth = norm_g.shape[0]
    xp = x_prompt.reshape(b * seq, dm)
    xs = x_sample.reshape(bd, dm)
    pool = lambda c: c.reshape(c.shape[0], c.shape[1], c.shape[2], -1)
    dk_pool, dv_pool = pool(cache_diff_k), pool(cache_diff_v)
    fk_pool, fv_pool = pool(cache_fox_k), pool(cache_fox_v)
    acc = {name: [] for name in ("dkp", "dvp", "dks", "dvs", "fkp", "fvp", "flp", "fks", "fvs", "fls",
                                 "srp", "sip", "srs", "sis")}
    for i in range(depth):
        kind, j = i % N_MIXERS, i // N_MIXERS
        final_g = final_norm_g if i == depth - 1 else None
        if kind == 0:
            xp, xs, kp, vp, ks, vs = _diff_layer(
                xp, xs, dk_pool, dv_pool, j, page_table, norm_g[i], diff_w_in[j], diff_lambda[j],
                diff_subln_g[j], diff_w_out[j], _diff_lambda_init(i), final_g, seq)
            acc["dkp"].append(kp); acc["dvp"].append(vp); acc["dks"].append(ks); acc["dvs"].append(vs)
        elif kind == 1:
            xp, xs, kp, vp, lfp, ks, vs, lfs = _fox_layer(
                xp, xs, fk_pool, fv_pool, cache_fox_logf, j, page_table, norm_g[i], fox_w_in[j], fox_b_f[j],
                fox_w_out[j], seq)
            acc["fkp"].append(kp); acc["fvp"].append(vp); acc["flp"].append(lfp)
            acc["fks"].append(ks); acc["fvs"].append(vs); acc["fls"].append(lfs)
        else:
            xp, xs, hrp, hip, hrs, his = _ssm_layer(
                xp, xs, state_ssm_re[j], state_ssm_im[j], norm_g[i], ssm_w_in[j], ssm_a_re[j], ssm_a_im[j],
                ssm_log_step[j], ssm_b_re[j], ssm_b_im[j], ssm_c_re[j], ssm_c_im[j], ssm_d[j], ssm_w_glu[j],
                ssm_w_out[j], seq)
            acc["srp"].append(hrp); acc["sip"].append(hip); acc["srs"].append(hrs); acc["sis"].append(his)
    ha, hb = cache_diff_k.shape[3], cache_fox_k.shape[3]
    stack = lambda name, shape: jnp.stack(acc[name]).reshape((len(acc[name]),) + shape)
    return (xp.reshape(b, seq, dm), xs.reshape(bd, 1, dm),
            stack("dkp", (b, seq, ha, dm // ha)), stack("dvp", (b, seq, ha, dm // ha)),
            stack("dks", (bd, 1, ha, dm // ha)), stack("dvs", (bd, 1, ha, dm // ha)),
            stack("fkp", (b, seq, hb, dm // hb)), stack("fvp", (b, seq, hb, dm // hb)),
            stack("flp", (b, seq, hb)),
            stack("fks", (bd, 1, hb, dm // hb)), stack("fvs", (bd, 1, hb, dm // hb)),
            stack("fls", (bd, 1, hb)),
            jnp.stack(acc["srp"]), jnp.stack(acc["sip"]), jnp.stack(acc["srs"]), jnp.stack(acc["sis"]))
```

```python
import functools
import math

import numpy as np

import jax
import jax.numpy as jnp
from jax import lax
from jax.experimental import pallas as pl
from jax.experimental.pallas import tpu as pltpu

F32 = jnp.float32
BF16 = jnp.bfloat16
HIGHEST = lax.Precision.HIGHEST

LANES = 128
SUBLANES = 8
D_MODEL = 1024
PAGE_SIZE = 128
N_MIXERS = 3
D_HEAD = 64
SSM_CH = 16
SSM_P = 64
SSM_CHUNK = 16
NORM_EPS = 1e-6
SUBLN_EPS = 1e-5
NEG_INF = -1e30
ROPE_THETA = 10000.0
LOG2E = math.log2(math.e)
QK_SCALE = D_HEAD ** -0.5 * LOG2E
BF16_ROWS = 16
VT_ROWS = LANES + BF16_ROWS
VMEM_LIMIT = 56 * 1024 * 1024

PROJ_ROWS = 256
OPROJ_ROWS = 512
CUMSUM_ROWS = 256
ATTN_TILE = 1024
ATTN_QSUB = 256

NT_DIMS = (((1,), (1,)), ((), ()))


def _params(*sem):
    return pltpu.CompilerParams(dimension_semantics=sem, vmem_limit_bytes=VMEM_LIMIT)


def _diff_lambda_init(layer_idx):
    return 0.8 - 0.6 * math.exp(-0.3 * layer_idx)


def _rms(x, g, eps):
    return x * lax.rsqrt(jnp.mean(x * x, axis=-1, keepdims=True) + eps) * g


def _silu(z):
    return z * jax.nn.sigmoid(z)


def _diff_lambda(lv, lam_init):
    return (jnp.exp(jnp.sum(lv[0:1] * lv[1:2], axis=1, keepdims=True))
            - jnp.exp(jnp.sum(lv[2:3] * lv[3:4], axis=1, keepdims=True)) + lam_init)


def _rope(a, cos, sin_signed, first_half):
    swapped = jnp.where(first_half, pltpu.roll(a, 96, 1), pltpu.roll(a, 32, 1))
    return a * cos + swapped * sin_signed


def _proj_body(*refs, mode, prompt):
    e = D_MODEL
    refs = list(refs)
    x_ref, g_ref, w_ref = refs[:3]
    refs = refs[3:]
    xn = _rms(x_ref[...], g_ref[...], NORM_EPS).astype(BF16)

    def proj(lo, width=e):
        return jnp.dot(xn, w_ref[:, lo:lo + width], preferred_element_type=F32)

    if mode == "ssm":
        u_ref, z_ref = refs
        u_ref[...] = proj(0)
        z_ref[...] = proj(e)
        return
    if mode == "diff":
        cos_ref, sin_ref = refs[:2]
        refs = refs[2:]
    else:
        bf_ref = refs.pop(0)
        lf_ref = refs.pop()
    if prompt:
        q_ref, k_ref, kb_ref, v_ref, vt_ref, z_ref = refs
    else:
        q_ref, k_ref, v_ref, z_ref = refs
    q = proj(0) * QK_SCALE
    k = proj(e)
    if mode == "diff":
        cos = cos_ref[...]
        sin = sin_ref[...]
        lane = lax.broadcasted_iota(jnp.int32, (1, LANES), 1)
        first_half = (lane % D_HEAD) < (D_HEAD // 2)
        blocks = [slice(h * LANES, (h + 1) * LANES) for h in range(e // LANES)]
        q = jnp.concatenate([_rope(q[:, sl], cos, sin, first_half) for sl in blocks], axis=1)
        k = jnp.concatenate([_rope(k[:, sl], cos, sin, first_half) for sl in blocks], axis=1)
    q_ref[...] = q.astype(q_ref.dtype)
    k_ref[...] = k
    v = proj(2 * e)
    v_ref[...] = v
    if prompt:
        kb_ref[...] = k.astype(BF16)
        ones = jnp.ones((BF16_ROWS, v.shape[0]), BF16)
        for h in range(e // LANES):
            vt_ref[h * VT_ROWS:h * VT_ROWS + LANES, :] = v[:, h * LANES:(h + 1) * LANES].T.astype(BF16)
            vt_ref[h * VT_ROWS + LANES:(h + 1) * VT_ROWS, :] = ones
    z_ref[...] = proj(3 * e)
    if mode == "fox":
        n_heads = lf_ref.shape[-1]
        f = proj(4 * e, LANES)[:, :n_heads] + bf_ref[...]
        lf_ref[...] = jax.nn.log_sigmoid(f)


def _proj(x, g, w, mode, extras, prompt):
    t, d = x.shape
    n = w.shape[1]
    tm = min(t, PROJ_ROWS)
    row = lambda i: (i, 0)
    fixed = lambda i: (0, 0)
    in_specs = [pl.BlockSpec((tm, d), row), pl.BlockSpec((1, d), fixed), pl.BlockSpec((d, n), fixed)]
    f32_out = (jax.ShapeDtypeStruct((t, d), F32), pl.BlockSpec((tm, d), row))
    bf_out = (jax.ShapeDtypeStruct((t, d), BF16), pl.BlockSpec((tm, d), row))
    if mode == "ssm":
        outs = [f32_out, f32_out]
    else:
        if mode == "diff":
            nblk = extras[0].shape[0] // tm
            tab = pl.BlockSpec((tm, LANES), lambda i: (i % nblk, 0))
            in_specs += [tab, tab]
        else:
            in_specs += [pl.BlockSpec(extras[0].shape, fixed)]
        if prompt:
            vt_rows = d // LANES * VT_ROWS
            vt_out = (jax.ShapeDtypeStruct((vt_rows, t), BF16), pl.BlockSpec((vt_rows, tm), lambda i: (0, i)))
            outs = [bf_out, f32_out, bf_out, f32_out, vt_out, f32_out]
        else:
            outs = [f32_out] * 4
        if mode == "fox":
            nh = extras[0].shape[1]
            outs.append((jax.ShapeDtypeStruct((t, nh), F32), pl.BlockSpec((tm, nh), row)))
    return pl.pallas_call(
        functools.partial(_proj_body, mode=mode, prompt=prompt),
        grid=(t // tm,),
        in_specs=in_specs,
        out_specs=[o[1] for o in outs],
        out_shape=[o[0] for o in outs],
        compiler_params=_params("parallel"),
        name=f"proj_{mode}",
    )(x, g.reshape(1, d), w, *extras)


def _rope_tables(pos):
    half = D_HEAD // 2
    inv = ROPE_THETA ** (-jnp.arange(half, dtype=F32) / half)
    ang = pos.astype(F32)[:, None] * inv[None, :]
    cos, sin = jnp.cos(ang), jnp.sin(ang)
    cos = jnp.concatenate([cos, cos, cos, cos], axis=1)
    sin = jnp.concatenate([-sin, sin, -sin, sin], axis=1)
    return cos, sin


def _colmax(st, parts=4):
    n = st.shape[0] // parts
    blocks = [st[i * n:(i + 1) * n] for i in range(parts)]
    while len(blocks) > 1:
        blocks = [jnp.maximum(blocks[i], blocks[i + 1]) for i in range(0, len(blocks), 2)]
    return jnp.max(blocks[0], axis=0, keepdims=True)


def _attn_body(qi_ref, kj_ref, *refs, kind, lam_init, tq, qsub):
    if kind == "diff":
        q_ref, k_ref, vt_ref, lam_ref, o_ref, m_sc, acc_sc = refs
    else:
        q_ref, qx0_ref, qx1_ref, k_ref, kx_ref, vt_ref, o_ref, m_sc, acc_sc = refs
    n = pl.program_id(2)
    qi = qi_ref[n]
    kj = kj_ref[n]
    lane = lax.broadcasted_iota(jnp.int32, (1, LANES), 1)
    low = lane < D_HEAD

    @pl.when(kj == 0)
    def _init():
        m_sc[...] = jnp.full(m_sc.shape, NEG_INF, F32)
        acc_sc[...] = jnp.zeros(acc_sc.shape, F32)

    def step(diagonal):
        q = q_ref[...]
        k = k_ref[...]
        if kind == "fox":
            k = jnp.concatenate([k, kx_ref[...]], axis=1)
        vt = vt_ref[...]
        chains = []
        for s in range(2):
            qs = jnp.where(low if s == 0 else jnp.logical_not(low), q, jnp.zeros_like(q))
            if kind == "fox":
                qs = jnp.concatenate([qs, (qx0_ref if s == 0 else qx1_ref)[...]], axis=1)
            for j in range(tq // qsub):
                cs = slice(j * qsub, (j + 1) * qsub)
                rows = (j + 1) * qsub if diagonal else tq
                st = lax.dot_general(k[:rows], qs[cs], NT_DIMS, preferred_element_type=F32)
                if diagonal:
                    kpos = lax.broadcasted_iota(jnp.int32, st.shape, 0)
                    qpos = j * qsub + lax.broadcasted_iota(jnp.int32, st.shape, 1)
                    st = jnp.where(kpos <= qpos, st, NEG_INF)
                chains.append((s, cs, rows, st))
        for s, cs, rows, st in chains:
            m_prev = m_sc[s, :, cs]
            m_new = jnp.maximum(m_prev, _colmax(st))
            alpha = jnp.exp2(m_prev - m_new)
            p = jnp.exp2(st - m_new).astype(BF16)
            acc_sc[s, :, cs] = alpha * acc_sc[s, :, cs] + jnp.dot(vt[:, :rows], p, preferred_element_type=F32)
            m_sc[s, :, cs] = m_new

    @pl.when(kj < qi)
    def _full():
        step(False)

    @pl.when(kj == qi)
    def _diag():
        step(True)
        o = [acc_sc[s, :LANES, :] / acc_sc[s, LANES:LANES + 1, :] for s in range(2)]
        if kind == "diff":
            w = o[0] - _diff_lambda(lam_ref[...], lam_init) * o[1]
        else:
            row = lax.broadcasted_iota(jnp.int32, o[0].shape, 0)
            w = jnp.where(row < D_HEAD, o[0], o[1])
        o_ref[...] = w.T


def _attn_prompt(qb, kb, vt, kind, extras, lam_init, seq):
    b, l, e = qb.shape
    tq, qsub = min(ATTN_TILE, seq), min(ATTN_QSUB, seq)
    nblk = l // tq
    steps = [(i, j) for i in range(nblk) for j in range(i + 1)]
    qi, kj = (jnp.asarray([s[c] for s in steps], jnp.int32) for c in range(2))
    q_spec = pl.BlockSpec((None, tq, LANES), lambda bb, h, n, qi, kj: (bb, qi[n], h))
    k_spec = pl.BlockSpec((None, tq, LANES), lambda bb, h, n, qi, kj: (bb, kj[n], h))
    vt_spec = pl.BlockSpec((VT_ROWS, tq), lambda bb, h, n, qi, kj: (h, bb * nblk + kj[n]))
    if kind == "diff":
        (lam,) = extras
        args = [qb, kb, vt, lam]
        in_specs = [q_spec, k_spec, vt_spec, pl.BlockSpec(lam.shape, lambda bb, h, n, qi, kj: (0, 0))]
    else:
        kx, qx0, qx1 = extras
        args = [qb, qx0, qx1, kb, kx, vt]
        in_specs = [q_spec, q_spec, q_spec, k_spec, k_spec, vt_spec]
    return pl.pallas_call(
        functools.partial(_attn_body, kind=kind, lam_init=lam_init, tq=tq, qsub=qsub),
        grid_spec=pltpu.PrefetchScalarGridSpec(
            num_scalar_prefetch=2,
            grid=(b, e // LANES, len(steps)),
            in_specs=in_specs,
            out_specs=q_spec,
            scratch_shapes=[pltpu.VMEM((2, 1, tq), F32), pltpu.VMEM((2, VT_ROWS, tq), F32)]),
        out_shape=jax.ShapeDtypeStruct((b, l, e), F32),
        compiler_params=_params("parallel", "parallel", "arbitrary"),
        name=f"attn_{kind}",
    )(qi, kj, *args)


def _fox_bias_body(lf_ref, sel_ref, const_ref, kx_ref, qx0_ref, qx1_ref, carry_sc):
    @pl.when(pl.program_id(1) == 0)
    def _():
        carry_sc[...] = jnp.zeros(carry_sc.shape, F32)

    lf = lf_ref[...]
    t = lf.shape[0]
    r = lax.broadcasted_iota(jnp.int32, (t, t), 0)
    c = lax.broadcasted_iota(jnp.int32, (t, t), 1)
    tri = (c <= r).astype(F32)
    cs = jnp.dot(tri, lf, preferred_element_type=F32, precision=HIGHEST) + carry_sc[...]
    carry_sc[...] = cs[t - 1:t, :]
    c2 = cs * LOG2E
    hi = c2.astype(BF16)
    rest = c2 - hi.astype(F32)
    mid = rest.astype(BF16)
    lo = (rest - mid.astype(F32)).astype(BF16)
    for o, out_ref in enumerate((kx_ref, qx0_ref, qx1_ref)):
        placed = const_ref[o]
        for piece, val in enumerate((hi, mid, lo)):
            placed = placed + jnp.dot(val, sel_ref[o, piece], preferred_element_type=F32)
        out_ref[...] = placed.astype(BF16)


def _fox_bias_layout(nh):
    sel = np.zeros((3, 3, nh, nh // 2 * LANES), np.float32)
    const = np.zeros((3, 1, nh // 2 * LANES), np.float32)
    for head in range(nh):
        base, s = head // 2 * LANES, head % 2
        for piece in range(3):
            sel[0, piece, head, base + 3 * s + piece] = 1.0
            sel[1 + s, piece, head, base + 6 + piece] = 1.0
            const[0, 0, base + 6 + piece] = 1.0
            const[1 + s, 0, base + 3 * s + piece] = -1.0
    return jnp.asarray(sel, BF16), jnp.asarray(const, F32)


def _fox_bias(logf):
    b, l, nh = logf.shape
    tm = min(l, CUMSUM_ROWS)
    sel, const = _fox_bias_layout(nh)
    width = sel.shape[-1]
    out = (jax.ShapeDtypeStruct((b, l, width), BF16), pl.BlockSpec((None, tm, width), lambda bb, i: (bb, i, 0)))
    return pl.pallas_call(
        _fox_bias_body,
        grid=(b, l // tm),
        in_specs=[pl.BlockSpec((None, tm, nh), lambda bb, i: (bb, i, 0)),
                  pl.BlockSpec(sel.shape, lambda bb, i: (0, 0, 0, 0)),
                  pl.BlockSpec(const.shape, lambda bb, i: (0, 0, 0))],
        out_specs=[out[1]] * 3,
        out_shape=[out[0]] * 3,
        scratch_shapes=[pltpu.VMEM((1, nh), F32)],
        compiler_params=_params("parallel", "arbitrary"),
        name="fox_bias",
    )(logf, sel, const)


def _oproj_body(*refs, mode, lam_init, final):
    refs = list(refs)
    a_ref, z_ref, x_ref = refs[:3]
    refs = refs[3:]
    if mode == "ssm":
        wg_ref = refs.pop(0)
    w_ref = refs.pop(0)
    if mode == "diff":
        sg_ref = refs.pop(0)
    if final:
        fg_ref = refs.pop(0)
    out_ref = refs.pop(0)
    e = D_MODEL
    z = z_ref[...]
    if mode == "diff":
        a = a_ref[...]
        sg = sg_ref[...] * (1.0 - lam_init)
        a = jnp.concatenate(
            [_rms(a[:, h * LANES:(h + 1) * LANES], 1.0, SUBLN_EPS) * sg for h in range(e // LANES)], axis=1)
    elif mode == "ssm":
        gl = jnp.dot(a_ref[...], wg_ref[...], preferred_element_type=F32)
        a = gl[:, :e] * jax.nn.sigmoid(gl[:, e:])
    else:
        a = a_ref[...]
    y = jnp.dot((a * _silu(z)).astype(BF16), w_ref[...], preferred_element_type=F32)
    xn = x_ref[...] + y
    if final:
        out_ref[...] = _rms(xn, fg_ref[...], NORM_EPS)
    else:
        out_ref[...] = xn


def _oproj(a, z, x, w, mode, wg=None, subln_g=None, lam_init=0.0, final_g=None):
    t, d = x.shape
    tm = min(t, OPROJ_ROWS)
    row = lambda i: (i, 0)
    fixed = lambda i: (0, 0)
    blk = pl.BlockSpec((tm, d), row)
    args = [a, z, x]
    in_specs = [blk, blk, blk]
    if mode == "ssm":
        args.append(wg)
        in_specs.append(pl.BlockSpec(wg.shape, fixed))
    args.append(w)
    in_specs.append(pl.BlockSpec(w.shape, fixed))
    if mode == "diff":
        args.append(subln_g.reshape(1, LANES))
        in_specs.append(pl.BlockSpec((1, LANES), fixed))
    if final_g is not None:
        args.append(final_g.reshape(1, d))
        in_specs.append(pl.BlockSpec((1, d), fixed))
    return pl.pallas_call(
        functools.partial(_oproj_body, mode=mode, lam_init=lam_init, final=final_g is not None),
        grid=(t // tm,),
        in_specs=in_specs,
        out_specs=blk,
        out_shape=jax.ShapeDtypeStruct((t, d), F32),
        compiler_params=_params("parallel"),
        name=f"oproj_{mode}",
    )(*args)


def _decode_diff_body(pt_ref, q_ref, kn_ref, vn_ref, lam_ref, *refs, lam_init, n_pages):
    del pt_ref
    k_refs = refs[:n_pages]
    v_refs = refs[n_pages:2 * n_pages]
    o_ref = refs[2 * n_pages]
    q8 = q_ref[...]
    nh = q8.shape[0]
    low = lax.broadcasted_iota(jnp.int32, q8.shape, 1) < D_HEAD
    q = jnp.concatenate([jnp.where(low, q8, 0.0), jnp.where(low, 0.0, q8)], axis=0)
    qb = q.astype(BF16)
    rows = k_refs[0].shape[0]
    own = (lax.broadcasted_iota(jnp.int32, (2 * nh, rows), 1) % nh
           == lax.broadcasted_iota(jnp.int32, (2 * nh, rows), 0) % nh)
    s = jnp.concatenate(
        [jnp.where(own, lax.dot_general(qb, k_refs[p][...].astype(BF16), NT_DIMS, preferred_element_type=F32),
                   NEG_INF) for p in range(n_pages)], axis=1)
    kn = kn_ref[...]
    vn = vn_ref[...]
    s_new = jnp.sum(q * jnp.concatenate([kn, kn], axis=0), axis=1, keepdims=True)
    m = jnp.maximum(jnp.max(s, axis=1, keepdims=True), s_new)
    p_past = jnp.exp2(s - m)
    p_new = jnp.exp2(s_new - m)
    l = jnp.sum(p_past, axis=1, keepdims=True) + p_new
    pb = p_past.astype(BF16)
    acc = p_new * jnp.concatenate([vn, vn], axis=0)
    for p in range(n_pages):
        acc = acc + jnp.dot(pb[:, p * rows:(p + 1) * rows], v_refs[p][...].astype(BF16),
                            preferred_element_type=F32)
    o = acc / l
    o_ref[...] = o[:nh] - _diff_lambda(lam_ref[...], lam_init) * o[nh:]


def _decode_fox_body(pt_ref, q_ref, kn_ref, vn_ref, lfn_ref, *refs, n_pages):
    del pt_ref
    kt_refs = refs[:n_pages]
    vt_refs = refs[n_pages:2 * n_pages]
    lf_refs = refs[2 * n_pages:3 * n_pages]
    o_ref = refs[3 * n_pages]
    e = q_ref.shape[-1]
    nh = e // D_HEAD
    b = pl.program_id(0)
    row = lax.broadcasted_iota(jnp.int32, (nh, e), 0)
    lane = lax.broadcasted_iota(jnp.int32, (nh, e), 1)
    own = (lane // D_HEAD) == row
    q = jnp.where(own, q_ref[...], 0.0)
    qb = q.astype(BF16)
    s = jnp.concatenate(
        [jnp.dot(qb, kt_refs[p][...].astype(BF16), preferred_element_type=F32) for p in range(n_pages)], axis=1)
    s_new = jnp.sum(q * kn_ref[...], axis=1, keepdims=True)
    page = lf_refs[0].shape[1]
    lft = jnp.concatenate([lf_refs[p][...] for p in range(n_pages)], axis=0)
    r = lax.broadcasted_iota(jnp.int32, (page, page), 0)
    c = lax.broadcasted_iota(jnp.int32, (page, page), 1)
    within = jnp.dot(lft, (r <= c).astype(F32), preferred_element_type=F32, precision=HIGHEST)
    offs = jnp.zeros((nh, 1), F32)
    c_past = []
    for p in range(n_pages):
        w = within[p * nh:(p + 1) * nh]
        c_past.append(w + offs)
        offs = offs + w[:, page - 1:page]
    c_past = jnp.concatenate(c_past, axis=1)
    lfn = lfn_ref[...]
    bcol = lax.broadcasted_iota(jnp.int32, lfn.shape, 1)
    c_new = offs + jnp.sum(jnp.where(bcol == b, lfn, 0.0), axis=1, keepdims=True)
    s = s + (c_new - c_past) * LOG2E
    m = jnp.maximum(jnp.max(s, axis=1, keepdims=True), s_new)
    p_past = jnp.exp2(s - m)
    p_new = jnp.exp2(s_new - m)
    l = jnp.sum(p_past, axis=1, keepdims=True) + p_new
    pb = p_past.astype(BF16)
    acc = p_new * vn_ref[...]
    for p in range(n_pages):
        acc = acc + lax.dot_general(pb[:, p * page:(p + 1) * page], vt_refs[p][...].astype(BF16), NT_DIMS,
                                    preferred_element_type=F32)
    o_ref[...] = jnp.sum(jnp.where(own, acc / l, 0.0), axis=0, keepdims=True)


def _decode_attn(q, k_new, v_new, k_pool, v_pool, page_table, kind, extras, lam_init, layer):
    bd, e = q.shape
    n_pages = page_table.shape[1]
    if kind == "diff":
        vec_shape = (bd, e // LANES, LANES)
    else:
        vec_shape = (bd, 1, e)
    vec = pl.BlockSpec((None,) + vec_shape[1:], lambda b, pt: (b, 0, 0))
    in_specs = [vec, vec, vec]
    args = [q.reshape(vec_shape), k_new.reshape(vec_shape), v_new.reshape(vec_shape)]
    pools = [k_pool, v_pool]
    if kind == "diff":
        (lam,) = extras
        args.append(lam)
        in_specs.append(pl.BlockSpec(lam.shape, lambda b, pt: (0, 0)))
        body = functools.partial(_decode_diff_body, lam_init=lam_init, n_pages=n_pages)
    else:
        lf_new_t, lf_pool = extras
        args.append(lf_new_t)
        in_specs.append(pl.BlockSpec(lf_new_t.shape, lambda b, pt: (0, 0)))
        pools.append(lf_pool)
        body = functools.partial(_decode_fox_body, n_pages=n_pages)

    def page_spec(pool, p):
        return pl.BlockSpec((None, None) + pool.shape[2:], lambda b, pt: (layer, pt[b, p], 0, 0))

    for pool in pools:
        args += [pool] * n_pages
        in_specs += [page_spec(pool, p) for p in range(n_pages)]
    out = pl.pallas_call(
        body,
        grid_spec=pltpu.PrefetchScalarGridSpec(
            num_scalar_prefetch=1,
            grid=(bd,),
            in_specs=in_specs,
            out_specs=vec),
        out_shape=jax.ShapeDtypeStruct(vec_shape, F32),
        compiler_params=_params("parallel"),
        name=f"decode_{kind}",
    )(page_table, *args)
    return out.reshape(bd, e)


def _ssm_disc_body(are_ref, aim_ref, dt_ref, pr_ref, pi_ref, fre_ref, fim_ref):
    a_re = jnp.minimum(are_ref[...], -1e-4)
    a_im = aim_ref[...]
    dt = dt_ref[...]
    mag = jnp.exp(a_re * dt)
    ang = a_im * dt
    lb_re = mag * jnp.cos(ang)
    lb_im = mag * jnp.sin(ang)
    den = a_re * a_re + a_im * a_im
    num_re = lb_re - 1.0
    num_im = lb_im
    fre_ref[...] = (num_re * a_re + num_im * a_im) / den
    fim_ref[...] = (num_im * a_re - num_re * a_im) / den
    pr = jnp.ones_like(lb_re)
    pi = jnp.zeros_like(lb_im)
    for j in range(SSM_CHUNK + 1):
        pr_ref[j] = pr
        pi_ref[j] = pi
        pr, pi = pr * lb_re - pi * lb_im, pr * lb_im + pi * lb_re


def _twice(a):
    return jnp.concatenate([a, a], axis=-1).astype(F32)


def _ssm_discretize(a_re, a_im, log_step):
    g, p = a_re.shape
    dt = jnp.broadcast_to(jnp.exp(log_step.astype(F32))[:, None], (g, 2 * p))
    pw = jax.ShapeDtypeStruct((SSM_CHUNK + 1, g, 2 * p), F32)
    gp = jax.ShapeDtypeStruct((g, 2 * p), F32)
    return pl.pallas_call(_ssm_disc_body, out_shape=[pw, pw, gp, gp], name="ssm_discretize")(
        _twice(a_re), _twice(a_im), dt)


def _ssm_tables_body(pr_ref, pi_ref, fre_ref, fim_ref, br_ref, bi_ref, cr_ref, ci_ref, sel_ref,
                     t_ref, e_ref, fr_ref, fi_ref, bb_ref):
    n = SSM_CHUNK
    ch = SSM_CH
    low = lax.broadcasted_iota(jnp.int32, (1, 2 * SSM_P), 1) < SSM_P
    fre = fre_ref[...]
    fim = fim_ref[...]
    br = br_ref[...]
    bi = bi_ref[...]
    bbr = fre * br - fim * bi
    bbi = fre * bi + fim * br
    bb_ref[...] = jnp.where(low, bbr, bbi)
    cr = cr_ref[...]
    ci = ci_ref[...]
    x_rows = []
    for j in range(n):
        pr = pr_ref[n - 1 - j]
        pi = pi_ref[n - 1 - j]
        x_rows.append(bbr * jnp.where(low, pr, pi) + bbi * jnp.where(low, -pi, pr))
    x_all = jnp.concatenate(x_rows, axis=0)
    sel = sel_ref[...]
    e_ref[...] = jnp.dot(x_all, sel, preferred_element_type=F32, precision=HIGHEST)
    c_cat = jnp.where(low, cr, -ci)
    c_rep = jnp.concatenate([c_cat] * n, axis=0)
    k_rev = lax.dot_general(x_all, c_rep, NT_DIMS, preferred_element_type=F32, precision=HIGHEST)
    blk = lax.broadcasted_iota(jnp.int32, (n * ch, n * ch), 1) // ch
    toe = jnp.zeros((n * ch, n * ch), F32)
    for t in range(n):
        sh = ch * (n - 1 - t)
        shifted = k_rev if sh == 0 else jnp.concatenate([k_rev[sh:], jnp.zeros((sh, n * ch), F32)], axis=0)
        toe = jnp.where(blk == t, shifted, toe)
    t_ref[...] = toe
    f_rows = []
    for t in range(n):
        pr = pr_ref[t + 1]
        pi = pi_ref[t + 1]
        f_rows.append(cr * jnp.where(low, pr, -pi) + ci * jnp.where(low, -pi, -pr))
    f_all = jnp.dot(jnp.concatenate(f_rows, axis=0), sel, preferred_element_type=F32, precision=HIGHEST)
    p2 = 2 * SSM_P
    fr_ref[...] = f_all[:, :p2]
    fi_ref[...] = f_all[:, p2:]


def _ssm_tables(pw_re, pw_im, f_re, f_im, b_re, b_im, c_re, c_im):
    g, p2 = f_re.shape
    p = p2 // 2
    n, ch = SSM_CHUNK, SSM_CH
    nc = n * ch
    eye = jnp.eye(p, dtype=F32)
    zero = jnp.zeros((p, p), F32)
    sel = jnp.stack([
        jnp.block([[eye, zero, zero, zero], [zero, zero, eye, zero]]),
        jnp.block([[zero, eye, zero, zero], [zero, zero, zero, eye]])])
    gmap = lambda i: (i, 0, 0)
    pw_spec = pl.BlockSpec((n + 1, None, 1, p2), lambda i: (0, i, 0, 0))
    row_spec = pl.BlockSpec((None, 1, p2), gmap)
    mat_spec = pl.BlockSpec((None, ch, p2), gmap)
    out = lambda rows, cols: (jax.ShapeDtypeStruct((g, rows, cols), F32), pl.BlockSpec((None, rows, cols), gmap))
    outs = [out(nc, nc), out(nc, 4 * p), out(nc, p2), out(nc, p2), out(ch, p2)]
    return pl.pallas_call(
        _ssm_tables_body,
        grid=(g,),
        in_specs=[pw_spec, pw_spec, row_spec, row_spec, mat_spec, mat_spec, mat_spec, mat_spec,
                  pl.BlockSpec((None, p2, 4 * p), lambda i: (i % 2, 0, 0))],
        out_specs=[o[1] for o in outs],
        out_shape=[o[0] for o in outs],
        compiler_params=_params("parallel"),
        name="ssm_tables",
    )(pw_re.reshape(n + 1, g, 1, p2), pw_im.reshape(n + 1, g, 1, p2), f_re.reshape(g, 1, p2), f_im.reshape(g, 1, p2),
      _twice(jnp.swapaxes(b_re, 1, 2)), _twice(jnp.swapaxes(b_im, 1, 2)), _twice(c_re), _twice(c_im), sel)


def _ssm_local_body(u_ref, e_ref, sre_ref, sim_ref):
    s = (jnp.dot(u_ref[0].astype(BF16), e_ref[0].astype(BF16), preferred_element_type=F32)
         + jnp.dot(u_ref[1].astype(BF16), e_ref[1].astype(BF16), preferred_element_type=F32))
    half = s.shape[1] // 2
    sre_ref[...] = s[:, :half]
    sim_ref[...] = s[:, half:]


def _ssm_local(ug, e_tab):
    g, nchunk, nc = ug.shape
    p = SSM_P
    st = jax.ShapeDtypeStruct((nchunk, g * p), F32)
    return pl.pallas_call(
        _ssm_local_body,
        grid=(g // 2,),
        in_specs=[pl.BlockSpec((2, nchunk, nc), lambda i: (i, 0, 0)),
                  pl.BlockSpec((2, nc, 4 * p), lambda i: (i, 0, 0))],
        out_specs=[pl.BlockSpec((nchunk, 2 * p), lambda i: (0, i))] * 2,
        out_shape=[st, st],
        compiler_params=_params("parallel"),
        name="ssm_local",
    )(ug, e_tab)


def _ssm_scan_body(sre_ref, sim_ref, ar_ref, ai_ref, hre_ref, him_ref, fre_ref, fim_ref, *, per_seq):
    n_seq = sre_ref.shape[0] // per_seq
    ar = ar_ref[...]
    ai = ai_ref[...]
    zero = jnp.zeros_like(ar)

    def body(i, carry):
        new = []
        for q in range(n_seq):
            hr, hi = carry[2 * q], carry[2 * q + 1]
            row = q * per_seq + i
            hre_ref[pl.ds(row, 1), :] = hr
            him_ref[pl.ds(row, 1), :] = hi
            new.append(ar * hr - ai * hi + sre_ref[pl.ds(row, 1), :])
            new.append(ar * hi + ai * hr + sim_ref[pl.ds(row, 1), :])
        return tuple(new)

    fin = lax.fori_loop(0, per_seq, body, (zero,) * (2 * n_seq))
    for q in range(n_seq):
        fre_ref[pl.ds(q, 1), :] = fin[2 * q]
        fim_ref[pl.ds(q, 1), :] = fin[2 * q + 1]


def _ssm_scan(s_re, s_im, a_re, a_im, n_seq):
    rows, width = s_re.shape
    wblk = 1024
    full = jax.ShapeDtypeStruct((rows, width), F32)
    fin = jax.ShapeDtypeStruct((n_seq, width), F32)
    col = lambda i: (0, i)
    return pl.pallas_call(
        functools.partial(_ssm_scan_body, per_seq=rows // n_seq),
        grid=(width // wblk,),
        in_specs=[pl.BlockSpec((rows, wblk), col)] * 2 + [pl.BlockSpec((1, wblk), col)] * 2,
        out_specs=[pl.BlockSpec((rows, wblk), col)] * 2 + [pl.BlockSpec((n_seq, wblk), col)] * 2,
        out_shape=[full, full, fin, fin],
        compiler_params=_params("parallel"),
        name="ssm_scan",
    )(s_re, s_im, a_re, a_im)


def _ssm_out_body(u_ref, t_ref, fr_ref, fi_ref, d_ref, hre_ref, him_ref, g_ref):
    hre = hre_ref[...].astype(BF16)
    him = him_ref[...].astype(BF16)
    for a in range(2):
        u = u_ref[a]
        y = jnp.dot(u.astype(BF16), t_ref[a].astype(BF16), preferred_element_type=F32)
        y = y + lax.dot_general(hre, fr_ref[a].astype(BF16), NT_DIMS, preferred_element_type=F32)
        y = y + lax.dot_general(him, fi_ref[a].astype(BF16), NT_DIMS, preferred_element_type=F32)
        y = y + d_ref[a] * u
        g_ref[a] = jax.nn.gelu(y).astype(BF16)


def _ssm_out(ug, t_tab, fr_tab, fi_tab, d_rows, h_re, h_im):
    g, nchunk, nc = ug.shape
    p = SSM_P
    pair3 = lambda i: (i, 0, 0)
    return pl.pallas_call(
        _ssm_out_body,
        grid=(g // 2,),
        in_specs=[pl.BlockSpec((2, nchunk, nc), pair3),
                  pl.BlockSpec((2, nc, nc), pair3),
                  pl.BlockSpec((2, nc, 2 * p), pair3),
                  pl.BlockSpec((2, nc, 2 * p), pair3),
                  pl.BlockSpec((2, 1, nc), pair3),
                  pl.BlockSpec((nchunk, 2 * p), lambda i: (0, i)),
                  pl.BlockSpec((nchunk, 2 * p), lambda i: (0, i))],
        out_specs=pl.BlockSpec((2, nchunk, nc), pair3),
        out_shape=jax.ShapeDtypeStruct((g, nchunk, nc), BF16),
        compiler_params=_params("parallel"),
        name="ssm_out",
    )(ug, t_tab, fr_tab, fi_tab, d_rows, h_re, h_im)


def _ssm_step_body(u_ref, hr_ref, hi_ref, lr_ref, li_ref, wb_ref, cr_ref, ci_ref, d_ref,
                   g_ref, nr_ref, ni_ref):
    u = u_ref[...]
    bu = jnp.dot(u.astype(BF16), wb_ref[...].astype(BF16), preferred_element_type=F32)
    half = bu.shape[1] // 2
    lr, li, hr, hi = lr_ref[...], li_ref[...], hr_ref[...], hi_ref[...]
    nr = lr * hr - li * hi + bu[:, :half]
    ni = lr * hi + li * hr + bu[:, half:]
    nr_ref[...] = nr
    ni_ref[...] = ni
    y = (jnp.dot(nr.astype(BF16), cr_ref[...].astype(BF16), preferred_element_type=F32)
         - jnp.dot(ni.astype(BF16), ci_ref[...].astype(BF16), preferred_element_type=F32)
         + d_ref[...] * u)
    g_ref[...] = jax.nn.gelu(y).astype(BF16)


def _ssm_step(u, h_re, h_im, lb_re, lb_im, bb, c_re, c_im, d):
    bd, e = u.shape
    g, ch, p2 = bb.shape
    p = p2 // 2
    gs = LANES // ch
    ns = g // gs
    eye = jnp.eye(gs, dtype=F32)
    wb = jnp.einsum("sgcrp,gh->sgcrhp", bb.reshape(ns, gs, ch, 2, p), eye).reshape(ns, LANES, 2 * gs * p)
    crb = jnp.einsum("sgcp,gh->shpgc", c_re.reshape(ns, gs, ch, p), eye).reshape(ns, gs * p, LANES)
    cib = jnp.einsum("sgcp,gh->shpgc", c_im.reshape(ns, gs, ch, p), eye).reshape(ns, gs * p, LANES)
    st = jax.ShapeDtypeStruct((bd, g * p), F32)
    col = lambda i: (0, i)
    slab = lambda i: (i, 0, 0)
    return pl.pallas_call(
        _ssm_step_body,
        grid=(ns,),
        in_specs=[pl.BlockSpec((bd, LANES), col),
                  pl.BlockSpec((bd, gs * p), col), pl.BlockSpec((bd, gs * p), col),
                  pl.BlockSpec((1, gs * p), col), pl.BlockSpec((1, gs * p), col),
                  pl.BlockSpec((None, LANES, 2 * gs * p), slab),
                  pl.BlockSpec((None, gs * p, LANES), slab), pl.BlockSpec((None, gs * p, LANES), slab),
                  pl.BlockSpec((1, LANES), col)],
        out_specs=[pl.BlockSpec((bd, LANES), col),
                   pl.BlockSpec((bd, gs * p), col), pl.BlockSpec((bd, gs * p), col)],
        out_shape=[jax.ShapeDtypeStruct((bd, e), BF16), st, st],
        compiler_params=_params("parallel"),
        name="ssm_step",
    )(u, h_re, h_im, lb_re, lb_im, wb, crb, cib, d.reshape(1, g * ch))


def _diff_layer(xp, xs, k_pool, v_pool, j, page_table, g, w_in, lam_vec, subln_g, w_out, lam_init, final_g, seq):
    wb = w_in.astype(BF16)
    wo = w_out.astype(BF16)
    lam = lam_vec.astype(F32)
    b = xp.shape[0] // seq
    e = xp.shape[1]
    past = page_table.shape[1] * PAGE_SIZE
    qbp, kp, kbp, vp, vtp, zp = _proj(xp, g, wb, "diff", _rope_tables(jnp.arange(seq)), True)
    qs, ks, vs, zs = _proj(xs, g, wb, "diff", _rope_tables(jnp.full((xs.shape[0],), past)), False)
    r3 = lambda a: a.reshape(b, seq, e)
    op = _attn_prompt(r3(qbp), r3(kbp), vtp, "diff", (lam,), lam_init, seq).reshape(b * seq, e)
    os_ = _decode_attn(qs, ks, vs, k_pool, v_pool, page_table, "diff", (lam,), lam_init, j)
    xp = _oproj(op, zp, xp, wo, "diff", subln_g=subln_g, lam_init=lam_init, final_g=final_g)
    xs = _oproj(os_, zs, xs, wo, "diff", subln_g=subln_g, lam_init=lam_init, final_g=final_g)
    return xp, xs, kp, vp, ks, vs


def _fox_layer(xp, xs, k_pool, v_pool, lf_pool, j, page_table, g, w_in, b_f, w_out, seq):
    nh = b_f.shape[0]
    wb = jnp.pad(w_in, ((0, 0), (0, LANES - nh))).astype(BF16)
    wo = w_out.astype(BF16)
    bf = b_f.reshape(1, nh).astype(F32)
    b = xp.shape[0] // seq
    e = xp.shape[1]
    qbp, kp, kbp, vp, vtp, zp, lfp = _proj(xp, g, wb, "fox", (bf,), True)
    qs, ks, vs, zs, lfs = _proj(xs, g, wb, "fox", (bf,), False)
    r3 = lambda a: a.reshape(b, seq, e)
    bias = _fox_bias(lfp.reshape(b, seq, nh))
    op = _attn_prompt(r3(qbp), r3(kbp), vtp, "fox", bias, 0.0, seq).reshape(b * seq, e)
    os_ = _decode_attn(qs, ks, vs, k_pool, v_pool, page_table, "fox", (lfs.T, lf_pool), 0.0, j)
    xp = _oproj(op, zp, xp, wo, "fox")
    xs = _oproj(os_, zs, xs, wo, "fox")
    return xp, xs, kp, vp, lfp, ks, vs, lfs


def _ssm_layer(xp, xs, h0_re, h0_im, g, w_in, a_re, a_im, log_step, b_re, b_im, c_re, c_im, d, w_glu, w_out, seq):
    ng, p = a_re.shape
    ch, n = SSM_CH, SSM_CHUNK
    wb = w_in.astype(BF16)
    wg = w_glu.astype(BF16)
    wo = w_out.astype(BF16)
    b = xp.shape[0] // seq
    e = xp.shape[1]
    pw_re, pw_im, f_re, f_im = _ssm_discretize(a_re, a_im, log_step)
    t_tab, e_tab, fr_tab, fi_tab, bb = _ssm_tables(pw_re, pw_im, f_re, f_im, b_re, b_im, c_re, c_im)
    dd = d.astype(F32)
    flat = lambda a: a[:, :p].reshape(1, ng * p)
    up, zp = _proj(xp, g, wb, "ssm", (), True)
    nchunk = b * seq // n
    ug = up.reshape(nchunk, n, ng, ch).transpose(2, 0, 1, 3).reshape(ng, nchunk, n * ch)
    s_re, s_im = _ssm_local(ug, e_tab)
    h_re, h_im, fin_re, fin_im = _ssm_scan(s_re, s_im, flat(pw_re[n]), flat(pw_im[n]), b)
    d_rows = jnp.tile(dd[:, None, :], (1, n, 1)).reshape(ng, 1, n * ch)
    gg = _ssm_out(ug, t_tab, fr_tab, fi_tab, d_rows, h_re, h_im)
    gp = gg.reshape(ng, nchunk, n, ch).transpose(1, 2, 0, 3).reshape(b * seq, e)
    xp = _oproj(gp, zp, xp, wo, "ssm", wg=wg)
    us, zs = _proj(xs, g, wb, "ssm", (), False)
    bd = xs.shape[0]
    gs, ns_re, ns_im = _ssm_step(us, h0_re.reshape(bd, ng * p).astype(F32), h0_im.reshape(bd, ng * p).astype(F32),
                                 flat(pw_re[1]), flat(pw_im[1]), bb, c_re.astype(F32), c_im.astype(F32), dd)
    xs = _oproj(gs, zs, xs, wo, "ssm", wg=wg)
    return xp, xs, fin_re.reshape(b, ng, p), fin_im.reshape(b, ng, p), ns_re.reshape(bd, ng, p), ns_im.reshape(bd, ng, p)


def kernel(x_prompt, x_sample, cache_diff_k, cache_diff_v, cache_fox_k, cache_fox_v, cache_fox_logf, state_ssm_re, state_ssm_im, page_table, norm_g, final_norm_g, diff_w_in, diff_lambda, diff_subln_g, diff_w_out, fox_w_in, fox_b_f, fox_w_out, ssm_w_in, ssm_a_re, ssm_a_im, ssm_log_step, ssm_b_re, ssm_b_im, ssm_c_re, ssm_c_im, ssm_d, ssm_w_glu, ssm_w_out):
    b, seq, dm = x_prompt.shape
    bd = x_sample.shape[0]
    depth = norm_g.shape[0]
    xp = x_prompt.reshape(b * seq, dm)
    xs = x_sample.reshape(bd, dm)
    rows = lambda c: c.reshape(c.shape[0], c.shape[1], c.shape[2] * c.shape[3], c.shape[4])
    cols = lambda c: jnp.transpose(c, (0, 1, 3, 4, 2)).reshape(c.shape[0], c.shape[1], c.shape[3] * c.shape[4],
                                                                c.shape[2])
    dk_pool, dv_pool = rows(cache_diff_k), rows(cache_diff_v)
    fk_pool, fv_pool = cols(cache_fox_k), cols(cache_fox_v)
    fl_pool = jnp.transpose(cache_fox_logf, (0, 1, 3, 2))
    acc = {name: [] for name in ("dkp", "dvp", "dks", "dvs", "fkp", "fvp", "flp", "fks", "fvs", "fls",
                                 "srp", "sip", "srs", "sis")}
    for i in range(depth):
        kind, j = i % N_MIXERS, i // N_MIXERS
        final_g = final_norm_g if i == depth - 1 else None
        if kind == 0:
            xp, xs, kp, vp, ks, vs = _diff_layer(
                xp, xs, dk_pool, dv_pool, j, page_table, norm_g[i], diff_w_in[j], diff_lambda[j],
                diff_subln_g[j], diff_w_out[j], _diff_lambda_init(i), final_g, seq)
            acc["dkp"].append(kp); acc["dvp"].append(vp); acc["dks"].append(ks); acc["dvs"].append(vs)
        elif kind == 1:
            xp, xs, kp, vp, lfp, ks, vs, lfs = _fox_layer(
                xp, xs, fk_pool, fv_pool, fl_pool, j, page_table, norm_g[i], fox_w_in[j], fox_b_f[j],
                fox_w_out[j], seq)
            acc["fkp"].append(kp); acc["fvp"].append(vp); acc["flp"].append(lfp)
            acc["fks"].append(ks); acc["fvs"].append(vs); acc["fls"].append(lfs)
        else:
            xp, xs, hrp, hip, hrs, his = _ssm_layer(
                xp, xs, state_ssm_re[j], state_ssm_im[j], norm_g[i], ssm_w_in[j], ssm_a_re[j], ssm_a_im[j],
                ssm_log_step[j], ssm_b_re[j], ssm_b_im[j], ssm_c_re[j], ssm_c_im[j], ssm_d[j], ssm_w_glu[j],
                ssm_w_out[j], seq)
            acc["srp"].append(hrp); acc["sip"].append(hip); acc["srs"].append(hrs); acc["sis"].append(his)
    ha, hb = cache_diff_k.shape[3], cache_fox_k.shape[3]
    stack = lambda name, shape: jnp.stack(acc[name]).reshape((len(acc[name]),) + shape)
    return (xp.reshape(b, seq, dm), xs.reshape(bd, 1, dm),
            stack("dkp", (b, seq, ha, dm // ha)), stack("dvp", (b, seq, ha, dm // ha)),
            stack("dks", (bd, 1, ha, dm // ha)), stack("dvs", (bd, 1, ha, dm // ha)),
            stack("fkp", (b, seq, hb, dm // hb)), stack("fvp", (b, seq, hb, dm // hb)),
            stack("flp", (b, seq, hb)),
            stack("fks", (bd, 1, hb, dm // hb)), stack("fvs", (bd, 1, hb, dm // hb)),
            stack("fls", (bd, 1, hb)),
            jnp.stack(acc["srp"]), jnp.stack(acc["sip"]), jnp.stack(acc["srs"]), jnp.stack(acc["sis"]))
```

```python
import functools
import math

import numpy as np

import jax
import jax.numpy as jnp
from jax import lax
from jax.experimental import pallas as pl
from jax.experimental.pallas import tpu as pltpu

F32 = jnp.float32
BF16 = jnp.bfloat16
HIGHEST = lax.Precision.HIGHEST

LANES = 128
SUBLANES = 8
D_MODEL = 1024
PAGE_SIZE = 128
N_MIXERS = 3
D_HEAD = 64
SSM_CH = 16
SSM_P = 64
SSM_CHUNK = 16
NORM_EPS = 1e-6
SUBLN_EPS = 1e-5
NEG_INF = -1e30
ROPE_THETA = 10000.0
LOG2E = math.log2(math.e)
QK_SCALE = D_HEAD ** -0.5 * LOG2E
BF16_ROWS = 16
VT_ROWS = LANES + BF16_ROWS
VMEM_LIMIT = 56 * 1024 * 1024

PROJ_ROWS = 256
OPROJ_ROWS = 512
CUMSUM_ROWS = 256
SSM_CHUNK_ROWS = 256
ATTN_TILE = 2048
ATTN_QSUB = 256

NT_DIMS = (((1,), (1,)), ((), ()))


def _params(*sem):
    return pltpu.CompilerParams(dimension_semantics=sem, vmem_limit_bytes=VMEM_LIMIT)


def _diff_lambda_init(layer_idx):
    return 0.8 - 0.6 * math.exp(-0.3 * layer_idx)


def _rms(x, g, eps):
    return x * lax.rsqrt(jnp.mean(x * x, axis=-1, keepdims=True) + eps) * g


def _silu(z):
    return z * jax.nn.sigmoid(z)


def _diff_lambda(lv, lam_init):
    return (jnp.exp(jnp.sum(lv[0:1] * lv[1:2], axis=1, keepdims=True))
            - jnp.exp(jnp.sum(lv[2:3] * lv[3:4], axis=1, keepdims=True)) + lam_init)


def _rope(a, cos, sin_signed, first_half):
    swapped = jnp.where(first_half, pltpu.roll(a, 96, 1), pltpu.roll(a, 32, 1))
    return a * cos + swapped * sin_signed


def _proj_body(*refs, mode, prompt):
    e = D_MODEL
    refs = list(refs)
    x_ref, g_ref, w_ref = refs[:3]
    refs = refs[3:]
    xn = _rms(x_ref[...], g_ref[...], NORM_EPS).astype(BF16)

    def proj(lo, width=e):
        return jnp.dot(xn, w_ref[:, lo:lo + width], preferred_element_type=F32)

    if mode == "ssm":
        u_ref, z_ref = refs
        u_ref[...] = proj(0)
        z_ref[...] = proj(e)
        return
    if mode == "diff":
        cos_ref, sin_ref = refs[:2]
        refs = refs[2:]
    else:
        bf_ref = refs.pop(0)
        lf_ref = refs.pop()
    if prompt:
        q_ref, k_ref, kb_ref, v_ref, vt_ref, z_ref = refs[-6:]
    else:
        q_ref, k_ref, v_ref, z_ref = refs
    q = proj(0) * QK_SCALE
    k = proj(e)
    if mode == "diff":
        cos = cos_ref[...]
        sin = sin_ref[...]
        lane = lax.broadcasted_iota(jnp.int32, (1, LANES), 1)
        first_half = (lane % D_HEAD) < (D_HEAD // 2)
        blocks = [slice(h * LANES, (h + 1) * LANES) for h in range(e // LANES)]
        q = jnp.concatenate([_rope(q[:, sl], cos, sin, first_half) for sl in blocks], axis=1)
        k = jnp.concatenate([_rope(k[:, sl], cos, sin, first_half) for sl in blocks], axis=1)
    q_ref[...] = q.astype(q_ref.dtype)
    v = proj(2 * e)
    if not prompt:
        k_ref[...] = k
        v_ref[...] = v
    else:
        kb_ref[...] = k.astype(BF16)
        ones = jnp.ones((BF16_ROWS, v.shape[0]), BF16)
        for h in range(e // LANES):
            sl = slice(h * LANES, (h + 1) * LANES)
            v_t = v[:, sl].T
            vt_ref[h * VT_ROWS:h * VT_ROWS + LANES, :] = v_t.astype(BF16)
            vt_ref[h * VT_ROWS + LANES:(h + 1) * VT_ROWS, :] = ones
            if mode == "fox":
                k_ref[sl, :] = k[:, sl].T
                v_ref[sl, :] = v_t
            else:
                k_ref[:, h, :] = k[:, sl]
                v_ref[:, h, :] = v[:, sl]
    z_ref[...] = proj(3 * e)
    if mode == "fox":
        n_heads = lf_ref.shape[-1]
        f = proj(4 * e, LANES)[:, :n_heads] + bf_ref[...]
        lf_ref[...] = jax.nn.log_sigmoid(f)


def _proj(x, g, w, mode, extras, prompt, cache=None):
    t, d = x.shape
    n = w.shape[1]
    tm = min(t, PROJ_ROWS)
    row = lambda i: (i, 0)
    fixed = lambda i: (0, 0)
    args = [x, g.reshape(1, d), w, *extras]
    in_specs = [pl.BlockSpec((tm, d), row), pl.BlockSpec((1, d), fixed), pl.BlockSpec((d, n), fixed)]
    f32_out = (jax.ShapeDtypeStruct((t, d), F32), pl.BlockSpec((tm, d), row))
    bf_out = (jax.ShapeDtypeStruct((t, d), BF16), pl.BlockSpec((tm, d), row))
    aliases = {}
    if mode == "ssm":
        outs = [f32_out, f32_out]
    else:
        if mode == "diff":
            nblk = extras[0].shape[0] // tm
            tab = pl.BlockSpec((tm, LANES), lambda i: (i % nblk, 0))
            in_specs += [tab, tab]
        else:
            in_specs += [pl.BlockSpec(extras[0].shape, fixed)]
        if prompt:
            slot, n_slots, prev, batch = cache
            vt_rows = d // LANES * VT_ROWS
            vt_out = (jax.ShapeDtypeStruct((vt_rows, t), BF16), pl.BlockSpec((vt_rows, tm), lambda i: (0, i)))
            if mode == "fox":
                nb = t // batch // tm
                kv_out = (jax.ShapeDtypeStruct((n_slots, batch, d, t // batch), F32),
                          pl.BlockSpec((None, None, d, tm), lambda i: (slot, i // nb, 0, i % nb)))
            else:
                kv_out = (jax.ShapeDtypeStruct((n_slots, t, d // LANES, LANES), F32),
                          pl.BlockSpec((None, tm, d // LANES, LANES), lambda i: (slot, i, 0, 0)))
            outs = [bf_out, kv_out, bf_out, kv_out, vt_out, f32_out]
            if prev is not None:
                aliases = {len(args): 1, len(args) + 1: 3}
                args += list(prev)
                in_specs += [pl.BlockSpec(memory_space=pl.ANY)] * 2
        else:
            outs = [f32_out] * 4
        if mode == "fox":
            nh = extras[0].shape[1]
            outs.append((jax.ShapeDtypeStruct((t, nh), F32), pl.BlockSpec((tm, nh), row)))
    return pl.pallas_call(
        functools.partial(_proj_body, mode=mode, prompt=prompt),
        grid=(t // tm,),
        in_specs=in_specs,
        out_specs=[o[1] for o in outs],
        out_shape=[o[0] for o in outs],
        input_output_aliases=aliases,
        compiler_params=_params("parallel"),
        name=f"proj_{mode}",
    )(*args)


def _rope_tables(pos):
    half = D_HEAD // 2
    inv = ROPE_THETA ** (-jnp.arange(half, dtype=F32) / half)
    ang = pos.astype(F32)[:, None] * inv[None, :]
    cos, sin = jnp.cos(ang), jnp.sin(ang)
    cos = jnp.concatenate([cos, cos, cos, cos], axis=1)
    sin = jnp.concatenate([-sin, sin, -sin, sin], axis=1)
    return cos, sin


def _colmax(st, parts=4):
    n = st.shape[0] // parts
    blocks = [st[i * n:(i + 1) * n] for i in range(parts)]
    while len(blocks) > 1:
        blocks = [jnp.maximum(blocks[i], blocks[i + 1]) for i in range(0, len(blocks), 2)]
    return jnp.max(blocks[0], axis=0, keepdims=True)


def _attn_body(qi_ref, kj_ref, *refs, kind, lam_init, tq, qsub):
    if kind == "diff":
        q_ref, k_ref, vt_ref, lam_ref, o_ref, m_sc, acc_sc = refs
    else:
        q_ref, qx0_ref, qx1_ref, k_ref, kx_ref, vt_ref, o_ref, m_sc, acc_sc = refs
    n = pl.program_id(2)
    qi = qi_ref[n]
    kj = kj_ref[n]
    lane = lax.broadcasted_iota(jnp.int32, (1, LANES), 1)
    low = lane < D_HEAD

    @pl.when(kj == 0)
    def _init():
        m_sc[...] = jnp.full(m_sc.shape, NEG_INF, F32)
        acc_sc[...] = jnp.zeros(acc_sc.shape, F32)

    def step(diagonal):
        q = q_ref[...]
        k = k_ref[...]
        if kind == "fox":
            k = jnp.concatenate([k, kx_ref[...]], axis=1)
        vt = vt_ref[...]
        chains = []
        for s in range(2):
            qs = jnp.where(low if s == 0 else jnp.logical_not(low), q, jnp.zeros_like(q))
            if kind == "fox":
                qs = jnp.concatenate([qs, (qx0_ref if s == 0 else qx1_ref)[...]], axis=1)
            for j in range(tq // qsub):
                cs = slice(j * qsub, (j + 1) * qsub)
                rows = (j + 1) * qsub if diagonal else tq
                st = lax.dot_general(k[:rows], qs[cs], NT_DIMS, preferred_element_type=F32)
                if diagonal:
                    kpos = lax.broadcasted_iota(jnp.int32, st.shape, 0)
                    qpos = j * qsub + lax.broadcasted_iota(jnp.int32, st.shape, 1)
                    st = jnp.where(kpos <= qpos, st, NEG_INF)
                chains.append((s, cs, rows, st))
        for s, cs, rows, st in chains:
            m_prev = m_sc[s, :, cs]
            m_new = jnp.maximum(m_prev, _colmax(st))
            alpha = jnp.exp2(m_prev - m_new)
            p = jnp.exp2(st - m_new).astype(BF16)
            acc_sc[s, :, cs] = alpha * acc_sc[s, :, cs] + jnp.dot(vt[:, :rows], p, preferred_element_type=F32)
            m_sc[s, :, cs] = m_new

    @pl.when(kj < qi)
    def _full():
        step(False)

    @pl.when(kj == qi)
    def _diag():
        step(True)
        o = [acc_sc[s, :LANES, :] / acc_sc[s, LANES:LANES + 1, :] for s in range(2)]
        if kind == "diff":
            w = o[0] - _diff_lambda(lam_ref[...], lam_init) * o[1]
        else:
            row = lax.broadcasted_iota(jnp.int32, o[0].shape, 0)
            w = jnp.where(row < D_HEAD, o[0], o[1])
        o_ref[...] = w.T


def _attn_prompt(qb, kb, vt, kind, extras, lam_init, seq):
    b, l, e = qb.shape
    tq, qsub = min(ATTN_TILE, seq), min(ATTN_QSUB, seq)
    nblk = l // tq
    steps = [(i, j) for i in range(nblk) for j in range(i + 1)]
    qi, kj = (jnp.asarray([s[c] for s in steps], jnp.int32) for c in range(2))
    q_spec = pl.BlockSpec((None, tq, LANES), lambda bb, h, n, qi, kj: (bb, qi[n], h))
    k_spec = pl.BlockSpec((None, tq, LANES), lambda bb, h, n, qi, kj: (bb, kj[n], h))
    vt_spec = pl.BlockSpec((VT_ROWS, tq), lambda bb, h, n, qi, kj: (h, bb * nblk + kj[n]))
    if kind == "diff":
        (lam,) = extras
        args = [qb, kb, vt, lam]
        in_specs = [q_spec, k_spec, vt_spec, pl.BlockSpec(lam.shape, lambda bb, h, n, qi, kj: (0, 0))]
    else:
        kx, qx0, qx1 = extras
        args = [qb, qx0, qx1, kb, kx, vt]
        in_specs = [q_spec, q_spec, q_spec, k_spec, k_spec, vt_spec]
    return pl.pallas_call(
        functools.partial(_attn_body, kind=kind, lam_init=lam_init, tq=tq, qsub=qsub),
        grid_spec=pltpu.PrefetchScalarGridSpec(
            num_scalar_prefetch=2,
            grid=(b, e // LANES, len(steps)),
            in_specs=in_specs,
            out_specs=q_spec,
            scratch_shapes=[pltpu.VMEM((2, 1, tq), F32), pltpu.VMEM((2, VT_ROWS, tq), F32)]),
        out_shape=jax.ShapeDtypeStruct((b, l, e), F32),
        compiler_params=_params("parallel", "parallel", "arbitrary"),
        name=f"attn_{kind}",
    )(qi, kj, *args)


def _fox_bias_body(lf_ref, sel_ref, const_ref, kx_ref, qx0_ref, qx1_ref, carry_sc):
    @pl.when(pl.program_id(1) == 0)
    def _():
        carry_sc[...] = jnp.zeros(carry_sc.shape, F32)

    lf = lf_ref[...]
    t = lf.shape[0]
    r = lax.broadcasted_iota(jnp.int32, (t, t), 0)
    c = lax.broadcasted_iota(jnp.int32, (t, t), 1)
    tri = (c <= r).astype(F32)
    cs = jnp.dot(tri, lf, preferred_element_type=F32, precision=HIGHEST) + carry_sc[...]
    carry_sc[...] = cs[t - 1:t, :]
    c2 = cs * LOG2E
    hi = c2.astype(BF16)
    rest = c2 - hi.astype(F32)
    mid = rest.astype(BF16)
    lo = (rest - mid.astype(F32)).astype(BF16)
    for o, out_ref in enumerate((kx_ref, qx0_ref, qx1_ref)):
        placed = const_ref[o]
        for piece, val in enumerate((hi, mid, lo)):
            placed = placed + jnp.dot(val, sel_ref[o, piece], preferred_element_type=F32)
        out_ref[...] = placed.astype(BF16)


def _fox_bias_layout(nh):
    sel = np.zeros((3, 3, nh, nh // 2 * LANES), np.float32)
    const = np.zeros((3, 1, nh // 2 * LANES), np.float32)
    for head in range(nh):
        base, s = head // 2 * LANES, head % 2
        for piece in range(3):
            sel[0, piece, head, base + 3 * s + piece] = 1.0
            sel[1 + s, piece, head, base + 6 + piece] = 1.0
            const[0, 0, base + 6 + piece] = 1.0
            const[1 + s, 0, base + 3 * s + piece] = -1.0
    return jnp.asarray(sel, BF16), jnp.asarray(const, F32)


def _fox_bias(logf):
    b, l, nh = logf.shape
    tm = min(l, CUMSUM_ROWS)
    sel, const = _fox_bias_layout(nh)
    width = sel.shape[-1]
    out = (jax.ShapeDtypeStruct((b, l, width), BF16), pl.BlockSpec((None, tm, width), lambda bb, i: (bb, i, 0)))
    return pl.pallas_call(
        _fox_bias_body,
        grid=(b, l // tm),
        in_specs=[pl.BlockSpec((None, tm, nh), lambda bb, i: (bb, i, 0)),
                  pl.BlockSpec(sel.shape, lambda bb, i: (0, 0, 0, 0)),
                  pl.BlockSpec(const.shape, lambda bb, i: (0, 0, 0))],
        out_specs=[out[1]] * 3,
        out_shape=[out[0]] * 3,
        scratch_shapes=[pltpu.VMEM((1, nh), F32)],
        compiler_params=_params("parallel", "arbitrary"),
        name="fox_bias",
    )(logf, sel, const)


def _oproj_body(*refs, mode, lam_init, final):
    refs = list(refs)
    a_ref, z_ref, x_ref = refs[:3]
    refs = refs[3:]
    if mode == "ssm":
        wg_ref = refs.pop(0)
    w_ref = refs.pop(0)
    if mode == "diff":
        sg_ref = refs.pop(0)
    if final:
        fg_ref = refs.pop(0)
    out_ref = refs.pop(0)
    e = D_MODEL
    z = z_ref[...]
    if mode == "diff":
        a = a_ref[...]
        sg = sg_ref[...] * (1.0 - lam_init)
        a = jnp.concatenate(
            [_rms(a[:, h * LANES:(h + 1) * LANES], 1.0, SUBLN_EPS) * sg for h in range(e // LANES)], axis=1)
    elif mode == "ssm":
        gl = jnp.dot(a_ref[...].astype(BF16), wg_ref[...], preferred_element_type=F32)
        a = gl[:, :e] * jax.nn.sigmoid(gl[:, e:])
    else:
        a = a_ref[...]
    y = jnp.dot((a * _silu(z)).astype(BF16), w_ref[...], preferred_element_type=F32)
    xn = x_ref[...] + y
    if final:
        out_ref[...] = _rms(xn, fg_ref[...], NORM_EPS)
    else:
        out_ref[...] = xn


def _oproj(a, z, x, w, mode, wg=None, subln_g=None, lam_init=0.0, final_g=None):
    t, d = x.shape
    tm = min(t, OPROJ_ROWS)
    row = lambda i: (i, 0)
    fixed = lambda i: (0, 0)
    blk = pl.BlockSpec((tm, d), row)
    args = [a, z, x]
    in_specs = [blk, blk, blk]
    if mode == "ssm":
        args.append(wg)
        in_specs.append(pl.BlockSpec(wg.shape, fixed))
    args.append(w)
    in_specs.append(pl.BlockSpec(w.shape, fixed))
    if mode == "diff":
        args.append(subln_g.reshape(1, LANES))
        in_specs.append(pl.BlockSpec((1, LANES), fixed))
    if final_g is not None:
        args.append(final_g.reshape(1, d))
        in_specs.append(pl.BlockSpec((1, d), fixed))
    return pl.pallas_call(
        functools.partial(_oproj_body, mode=mode, lam_init=lam_init, final=final_g is not None),
        grid=(t // tm,),
        in_specs=in_specs,
        out_specs=blk,
        out_shape=jax.ShapeDtypeStruct((t, d), F32),
        compiler_params=_params("parallel"),
        name=f"oproj_{mode}",
    )(*args)


def _decode_diff_body(pt_ref, q_ref, kn_ref, vn_ref, lam_ref, *refs, lam_init, n_pages):
    del pt_ref
    k_refs = refs[:n_pages]
    v_refs = refs[n_pages:2 * n_pages]
    o_ref = refs[2 * n_pages]
    q8 = q_ref[...]
    nh = q8.shape[0]
    low = lax.broadcasted_iota(jnp.int32, q8.shape, 1) < D_HEAD
    q = jnp.concatenate([jnp.where(low, q8, 0.0), jnp.where(low, 0.0, q8)], axis=0)
    qb = q.astype(BF16)
    rows = k_refs[0].shape[0]
    own = (lax.broadcasted_iota(jnp.int32, (2 * nh, rows), 1) % nh
           == lax.broadcasted_iota(jnp.int32, (2 * nh, rows), 0) % nh)
    s = jnp.concatenate(
        [jnp.where(own, lax.dot_general(qb, k_refs[p][...].astype(BF16), NT_DIMS, preferred_element_type=F32),
                   NEG_INF) for p in range(n_pages)], axis=1)
    kn = kn_ref[...]
    vn = vn_ref[...]
    s_new = jnp.sum(q * jnp.concatenate([kn, kn], axis=0), axis=1, keepdims=True)
    m = jnp.maximum(jnp.max(s, axis=1, keepdims=True), s_new)
    p_past = jnp.exp2(s - m)
    p_new = jnp.exp2(s_new - m)
    l = jnp.sum(p_past, axis=1, keepdims=True) + p_new
    pb = p_past.astype(BF16)
    acc = p_new * jnp.concatenate([vn, vn], axis=0)
    for p in range(n_pages):
        acc = acc + jnp.dot(pb[:, p * rows:(p + 1) * rows], v_refs[p][...].astype(BF16),
                            preferred_element_type=F32)
    o = acc / l
    o_ref[...] = o[:nh] - _diff_lambda(lam_ref[...], lam_init) * o[nh:]


def _decode_fox_body(pt_ref, q_ref, kn_ref, vn_ref, lfn_ref, *refs, n_pages):
    del pt_ref
    kt_refs = refs[:n_pages]
    vt_refs = refs[n_pages:2 * n_pages]
    lf_refs = refs[2 * n_pages:3 * n_pages]
    o_ref = refs[3 * n_pages]
    e = q_ref.shape[-1]
    nh = e // D_HEAD
    b = pl.program_id(0)
    row = lax.broadcasted_iota(jnp.int32, (nh, e), 0)
    lane = lax.broadcasted_iota(jnp.int32, (nh, e), 1)
    own = (lane // D_HEAD) == row
    q = jnp.where(own, q_ref[...], 0.0)
    qb = q.astype(BF16)
    s = jnp.concatenate(
        [jnp.dot(qb, kt_refs[p][...].astype(BF16), preferred_element_type=F32) for p in range(n_pages)], axis=1)
    s_new = jnp.sum(q * kn_ref[...], axis=1, keepdims=True)
    page = lf_refs[0].shape[1]
    lft = jnp.concatenate([lf_refs[p][...] for p in range(n_pages)], axis=0)
    r = lax.broadcasted_iota(jnp.int32, (page, page), 0)
    c = lax.broadcasted_iota(jnp.int32, (page, page), 1)
    within = jnp.dot(lft, (r <= c).astype(F32), preferred_element_type=F32, precision=HIGHEST)
    offs = jnp.zeros((nh, 1), F32)
    c_past = []
    for p in range(n_pages):
        w = within[p * nh:(p + 1) * nh]
        c_past.append(w + offs)
        offs = offs + w[:, page - 1:page]
    c_past = jnp.concatenate(c_past, axis=1)
    lfn = lfn_ref[...]
    bcol = lax.broadcasted_iota(jnp.int32, lfn.shape, 1)
    c_new = offs + jnp.sum(jnp.where(bcol == b, lfn, 0.0), axis=1, keepdims=True)
    s = s + (c_new - c_past) * LOG2E
    m = jnp.maximum(jnp.max(s, axis=1, keepdims=True), s_new)
    p_past = jnp.exp2(s - m)
    p_new = jnp.exp2(s_new - m)
    l = jnp.sum(p_past, axis=1, keepdims=True) + p_new
    pb = p_past.astype(BF16)
    acc = p_new * vn_ref[...]
    for p in range(n_pages):
        acc = acc + lax.dot_general(pb[:, p * page:(p + 1) * page], vt_refs[p][...].astype(BF16), NT_DIMS,
                                    preferred_element_type=F32)
    o_ref[...] = jnp.sum(jnp.where(own, acc / l, 0.0), axis=0, keepdims=True)


def _decode_attn(q, k_new, v_new, k_pool, v_pool, page_table, kind, extras, lam_init, layer):
    bd, e = q.shape
    n_pages = page_table.shape[1]
    if kind == "diff":
        vec_shape = (bd, e // LANES, LANES)
    else:
        vec_shape = (bd, 1, e)
    vec = pl.BlockSpec((None,) + vec_shape[1:], lambda b, pt: (b, 0, 0))
    in_specs = [vec, vec, vec]
    args = [q.reshape(vec_shape), k_new.reshape(vec_shape), v_new.reshape(vec_shape)]
    pools = [k_pool, v_pool]
    if kind == "diff":
        (lam,) = extras
        args.append(lam)
        in_specs.append(pl.BlockSpec(lam.shape, lambda b, pt: (0, 0)))
        body = functools.partial(_decode_diff_body, lam_init=lam_init, n_pages=n_pages)
    else:
        lf_new_t, lf_pool = extras
        args.append(lf_new_t)
        in_specs.append(pl.BlockSpec(lf_new_t.shape, lambda b, pt: (0, 0)))
        pools.append(lf_pool)
        body = functools.partial(_decode_fox_body, n_pages=n_pages)

    def page_spec(pool, p):
        return pl.BlockSpec((None, None) + pool.shape[2:], lambda b, pt: (layer, pt[b, p], 0, 0))

    for pool in pools:
        args += [pool] * n_pages
        in_specs += [page_spec(pool, p) for p in range(n_pages)]
    out = pl.pallas_call(
        body,
        grid_spec=pltpu.PrefetchScalarGridSpec(
            num_scalar_prefetch=1,
            grid=(bd,),
            in_specs=in_specs,
            out_specs=vec),
        out_shape=jax.ShapeDtypeStruct(vec_shape, F32),
        compiler_params=_params("parallel"),
        name=f"decode_{kind}",
    )(page_table, *args)
    return out.reshape(bd, e)


def _ssm_disc_body(are_ref, aim_ref, dt_ref, pr_ref, pi_ref, fre_ref, fim_ref):
    a_re = jnp.minimum(are_ref[...], -1e-4)
    a_im = aim_ref[...]
    dt = dt_ref[...]
    mag = jnp.exp(a_re * dt)
    ang = a_im * dt
    lb_re = mag * jnp.cos(ang)
    lb_im = mag * jnp.sin(ang)
    den = a_re * a_re + a_im * a_im
    num_re = lb_re - 1.0
    num_im = lb_im
    fre_ref[...] = (num_re * a_re + num_im * a_im) / den
    fim_ref[...] = (num_im * a_re - num_re * a_im) / den
    pr = jnp.ones_like(lb_re)
    pi = jnp.zeros_like(lb_im)
    for j in range(SSM_CHUNK + 1):
        pr_ref[j] = pr
        pi_ref[j] = pi
        pr, pi = pr * lb_re - pi * lb_im, pr * lb_im + pi * lb_re


def _twice(a):
    return jnp.concatenate([a, a], axis=-1).astype(F32)


def _ssm_discretize(a_re, a_im, log_step):
    g, p = a_re.shape
    dt = jnp.broadcast_to(jnp.exp(log_step.astype(F32))[:, None], (g, 2 * p))
    pw = jax.ShapeDtypeStruct((SSM_CHUNK + 1, g, 2 * p), F32)
    gp = jax.ShapeDtypeStruct((g, 2 * p), F32)
    return pl.pallas_call(_ssm_disc_body, out_shape=[pw, pw, gp, gp], name="ssm_discretize")(
        _twice(a_re), _twice(a_im), dt)


def _ssm_tables_body(pr_ref, pi_ref, fre_ref, fim_ref, br_ref, bi_ref, cr_ref, ci_ref,
                     t_ref, x_ref, f_ref, bb_ref):
    n = SSM_CHUNK
    ch = SSM_CH
    low = lax.broadcasted_iota(jnp.int32, (1, 2 * SSM_P), 1) < SSM_P
    fre = fre_ref[...]
    fim = fim_ref[...]
    br = br_ref[...]
    bi = bi_ref[...]
    bbr = fre * br - fim * bi
    bbi = fre * bi + fim * br
    bb_ref[...] = jnp.where(low, bbr, bbi)
    cr = cr_ref[...]
    ci = ci_ref[...]
    x_rows = []
    for j in range(n):
        pr = pr_ref[n - 1 - j]
        pi = pi_ref[n - 1 - j]
        x_rows.append(bbr * jnp.where(low, pr, pi) + bbi * jnp.where(low, -pi, pr))
    x_all = jnp.concatenate(x_rows, axis=0)
    x_ref[...] = x_all
    c_cat = jnp.where(low, cr, -ci)
    c_rep = jnp.concatenate([c_cat] * n, axis=0)
    k_rev = lax.dot_general(x_all, c_rep, NT_DIMS, preferred_element_type=F32, precision=HIGHEST)
    blk = lax.broadcasted_iota(jnp.int32, (n * ch, n * ch), 1) // ch
    toe = jnp.zeros((n * ch, n * ch), F32)
    for t in range(n):
        sh = ch * (n - 1 - t)
        shifted = k_rev if sh == 0 else jnp.concatenate([k_rev[sh:], jnp.zeros((sh, n * ch), F32)], axis=0)
        toe = jnp.where(blk == t, shifted, toe)
    t_ref[...] = toe
    f_rows = []
    for t in range(n):
        pr = pr_ref[t + 1]
        pi = pi_ref[t + 1]
        f_rows.append(cr * jnp.where(low, pr, -pi) + ci * jnp.where(low, -pi, -pr))
    f_ref[...] = jnp.concatenate(f_rows, axis=0)


def _ssm_tables(pw_re, pw_im, f_re, f_im, b_re, b_im, c_re, c_im):
    g, p2 = f_re.shape
    n, ch = SSM_CHUNK, SSM_CH
    nc = n * ch
    gmap = lambda i: (i, 0, 0)
    pw_spec = pl.BlockSpec((n + 1, None, 1, p2), lambda i: (0, i, 0, 0))
    row_spec = pl.BlockSpec((None, 1, p2), gmap)
    mat_spec = pl.BlockSpec((None, ch, p2), gmap)
    out = lambda rows, cols: (jax.ShapeDtypeStruct((g, rows, cols), F32), pl.BlockSpec((None, rows, cols), gmap))
    outs = [out(nc, nc), out(nc, p2), out(nc, p2), out(ch, p2)]
    return pl.pallas_call(
        _ssm_tables_body,
        grid=(g,),
        in_specs=[pw_spec, pw_spec, row_spec, row_spec, mat_spec, mat_spec, mat_spec, mat_spec],
        out_specs=[o[1] for o in outs],
        out_shape=[o[0] for o in outs],
        compiler_params=_params("parallel"),
        name="ssm_tables",
    )(pw_re.reshape(n + 1, g, 1, p2), pw_im.reshape(n + 1, g, 1, p2), f_re.reshape(g, 1, p2), f_im.reshape(g, 1, p2),
      _twice(jnp.swapaxes(b_re, 1, 2)), _twice(jnp.swapaxes(b_im, 1, 2)), _twice(c_re), _twice(c_im))


def _slab_expand(tab, ns, gs, minor):
    g, rows, cols = tab.shape
    n, ch = SSM_CHUNK, SSM_CH
    w = tab.reshape(ns, gs, n, ch, cols).transpose(0, 2, 1, 3, 4).reshape(ns, n * gs * ch, cols)
    x = cols // minor
    spread = np.zeros((x, minor, x, gs, minor), np.float32)
    for i in range(x):
        for m in range(minor):
            spread[i, m, i, :, m] = 1.0
    spread = jnp.asarray(spread.reshape(cols, x * gs * minor), BF16)
    row_g = (np.arange(n * gs * ch) // ch) % gs
    col_g = (np.arange(x * gs * minor) // minor) % gs
    keep = jnp.asarray(row_g[:, None] == col_g[None, :])
    z = jnp.einsum("arc,cd->ard", w.astype(BF16), spread, preferred_element_type=BF16)
    return jnp.where(keep[None], z, jnp.zeros_like(z))


def _chunk_rows(u_ref, cb):
    return jnp.concatenate([u_ref[pl.ds(t, cb, stride=SSM_CHUNK), :] for t in range(SSM_CHUNK)], axis=1)


def _ssm_local_body(u_ref, e_ref, sre_ref, sim_ref):
    cb = sre_ref.shape[0]
    s = jnp.dot(_chunk_rows(u_ref, cb).astype(BF16), e_ref[...], preferred_element_type=F32)
    half = s.shape[1] // 2
    sre_ref[...] = s[:, :half]
    sim_ref[...] = s[:, half:]


def _ssm_local(u, e_slab, cb):
    t, e = u.shape
    ns, kdim, width = e_slab.shape
    nchunk = t // SSM_CHUNK
    half = width // 2
    st = jax.ShapeDtypeStruct((nchunk, ns * half), F32)
    return pl.pallas_call(
        _ssm_local_body,
        grid=(ns, nchunk // cb),
        in_specs=[pl.BlockSpec((cb * SSM_CHUNK, LANES), lambda a, i: (i, a)),
                  pl.BlockSpec((None, kdim, width), lambda a, i: (a, 0, 0))],
        out_specs=[pl.BlockSpec((cb, half), lambda a, i: (i, a))] * 2,
        out_shape=[st, st],
        compiler_params=_params("parallel", "parallel"),
        name="ssm_local",
    )(u, e_slab)


def _ssm_scan_body(sre_ref, sim_ref, ar_ref, ai_ref, hre_ref, him_ref, fre_ref, fim_ref, *, per_seq):
    n_seq = sre_ref.shape[0] // per_seq
    ar = ar_ref[...]
    ai = ai_ref[...]
    zero = jnp.zeros_like(ar)

    def body(i, carry):
        new = []
        for q in range(n_seq):
            hr, hi = carry[2 * q], carry[2 * q + 1]
            row = q * per_seq + i
            hre_ref[pl.ds(row, 1), :] = hr
            him_ref[pl.ds(row, 1), :] = hi
            new.append(ar * hr - ai * hi + sre_ref[pl.ds(row, 1), :])
            new.append(ar * hi + ai * hr + sim_ref[pl.ds(row, 1), :])
        return tuple(new)

    fin = lax.fori_loop(0, per_seq, body, (zero,) * (2 * n_seq))
    for q in range(n_seq):
        fre_ref[pl.ds(q, 1), :] = fin[2 * q]
        fim_ref[pl.ds(q, 1), :] = fin[2 * q + 1]


def _ssm_scan(s_re, s_im, a_re, a_im, n_seq):
    rows, width = s_re.shape
    wblk = 1024
    full = jax.ShapeDtypeStruct((rows, width), F32)
    fin = jax.ShapeDtypeStruct((n_seq, width), F32)
    col = lambda i: (0, i)
    return pl.pallas_call(
        functools.partial(_ssm_scan_body, per_seq=rows // n_seq),
        grid=(width // wblk,),
        in_specs=[pl.BlockSpec((rows, wblk), col)] * 2 + [pl.BlockSpec((1, wblk), col)] * 2,
        out_specs=[pl.BlockSpec((rows, wblk), col)] * 2 + [pl.BlockSpec((n_seq, wblk), col)] * 2,
        out_shape=[full, full, fin, fin],
        compiler_params=_params("parallel"),
        name="ssm_scan",
    )(s_re, s_im, a_re, a_im)


def _ssm_out_body(u_ref, t_ref, f_ref, d_ref, hre_ref, him_ref, g_ref):
    cb = hre_ref.shape[0]
    u = _chunk_rows(u_ref, cb)
    y = jnp.dot(u.astype(BF16), t_ref[...], preferred_element_type=F32)
    y = y + jnp.dot(hre_ref[...].astype(BF16), f_ref[0], preferred_element_type=F32)
    y = y + jnp.dot(him_ref[...].astype(BF16), f_ref[1], preferred_element_type=F32)
    g = jax.nn.gelu(y + d_ref[...] * u)
    for t in range(SSM_CHUNK):
        g_ref[pl.ds(t, cb, stride=SSM_CHUNK), :] = g[:, t * LANES:(t + 1) * LANES]


def _ssm_out(u, t_slab, f_slab, d_slab, h_re, h_im, cb):
    t, e = u.shape
    ns, kdim, _ = t_slab.shape
    half = f_slab.shape[2]
    rows = cb * SSM_CHUNK
    return pl.pallas_call(
        _ssm_out_body,
        grid=(ns, t // rows),
        in_specs=[pl.BlockSpec((rows, LANES), lambda a, i: (i, a)),
                  pl.BlockSpec((None, kdim, kdim), lambda a, i: (a, 0, 0)),
                  pl.BlockSpec((None, 2, half, kdim), lambda a, i: (a, 0, 0, 0)),
                  pl.BlockSpec((None, 1, kdim), lambda a, i: (a, 0, 0)),
                  pl.BlockSpec((cb, half), lambda a, i: (i, a)),
                  pl.BlockSpec((cb, half), lambda a, i: (i, a))],
        out_specs=pl.BlockSpec((rows, LANES), lambda a, i: (i, a)),
        out_shape=jax.ShapeDtypeStruct((t, e), F32),
        compiler_params=_params("parallel", "parallel"),
        name="ssm_out",
    )(u, t_slab, f_slab, d_slab, h_re, h_im)


def _ssm_step_body(u_ref, hr_ref, hi_ref, lr_ref, li_ref, wb_ref, cr_ref, ci_ref, d_ref,
                   g_ref, nr_ref, ni_ref):
    u = u_ref[...]
    bu = jnp.dot(u.astype(BF16), wb_ref[...].astype(BF16), preferred_element_type=F32)
    half = bu.shape[1] // 2
    lr, li, hr, hi = lr_ref[...], li_ref[...], hr_ref[...], hi_ref[...]
    nr = lr * hr - li * hi + bu[:, :half]
    ni = lr * hi + li * hr + bu[:, half:]
    nr_ref[...] = nr
    ni_ref[...] = ni
    y = (jnp.dot(nr.astype(BF16), cr_ref[...].astype(BF16), preferred_element_type=F32)
         - jnp.dot(ni.astype(BF16), ci_ref[...].astype(BF16), preferred_element_type=F32)
         + d_ref[...] * u)
    g_ref[...] = jax.nn.gelu(y).astype(BF16)


def _ssm_step(u, h_re, h_im, lb_re, lb_im, bb, c_re, c_im, d):
    bd, e = u.shape
    g, ch, p2 = bb.shape
    p = p2 // 2
    gs = LANES // ch
    ns = g // gs
    eye = jnp.eye(gs, dtype=F32)
    wb = jnp.einsum("sgcrp,gh->sgcrhp", bb.reshape(ns, gs, ch, 2, p), eye).reshape(ns, LANES, 2 * gs * p)
    crb = jnp.einsum("sgcp,gh->shpgc", c_re.reshape(ns, gs, ch, p), eye).reshape(ns, gs * p, LANES)
    cib = jnp.einsum("sgcp,gh->shpgc", c_im.reshape(ns, gs, ch, p), eye).reshape(ns, gs * p, LANES)
    st = jax.ShapeDtypeStruct((bd, g * p), F32)
    col = lambda i: (0, i)
    slab = lambda i: (i, 0, 0)
    return pl.pallas_call(
        _ssm_step_body,
        grid=(ns,),
        in_specs=[pl.BlockSpec((bd, LANES), col),
                  pl.BlockSpec((bd, gs * p), col), pl.BlockSpec((bd, gs * p), col),
                  pl.BlockSpec((1, gs * p), col), pl.BlockSpec((1, gs * p), col),
                  pl.BlockSpec((None, LANES, 2 * gs * p), slab),
                  pl.BlockSpec((None, gs * p, LANES), slab), pl.BlockSpec((None, gs * p, LANES), slab),
                  pl.BlockSpec((1, LANES), col)],
        out_specs=[pl.BlockSpec((bd, LANES), col),
                   pl.BlockSpec((bd, gs * p), col), pl.BlockSpec((bd, gs * p), col)],
        out_shape=[jax.ShapeDtypeStruct((bd, e), BF16), st, st],
        compiler_params=_params("parallel"),
        name="ssm_step",
    )(u, h_re, h_im, lb_re, lb_im, wb, crb, cib, d.reshape(1, g * ch))


def _diff_layer(xp, xs, k_pool, v_pool, j, page_table, g, w_in, lam_vec, subln_g, w_out, lam_init, final_g, seq,
                cache):
    wb = w_in.astype(BF16)
    wo = w_out.astype(BF16)
    lam = lam_vec.astype(F32)
    b = xp.shape[0] // seq
    e = xp.shape[1]
    past = page_table.shape[1] * PAGE_SIZE
    qbp, kp, kbp, vp, vtp, zp = _proj(xp, g, wb, "diff", _rope_tables(jnp.arange(seq)), True, cache)
    qs, ks, vs, zs = _proj(xs, g, wb, "diff", _rope_tables(jnp.full((xs.shape[0],), past)), False)
    r3 = lambda a: a.reshape(b, seq, e)
    op = _attn_prompt(r3(qbp), r3(kbp), vtp, "diff", (lam,), lam_init, seq).reshape(b * seq, e)
    os_ = _decode_attn(qs, ks, vs, k_pool, v_pool, page_table, "diff", (lam,), lam_init, j)
    xp = _oproj(op, zp, xp, wo, "diff", subln_g=subln_g, lam_init=lam_init, final_g=final_g)
    xs = _oproj(os_, zs, xs, wo, "diff", subln_g=subln_g, lam_init=lam_init, final_g=final_g)
    return xp, xs, kp, vp, ks, vs


def _fox_layer(xp, xs, k_pool, v_pool, lf_pool, j, page_table, g, w_in, b_f, w_out, seq, cache):
    nh = b_f.shape[0]
    wb = jnp.pad(w_in, ((0, 0), (0, LANES - nh))).astype(BF16)
    wo = w_out.astype(BF16)
    bf = b_f.reshape(1, nh).astype(F32)
    b = xp.shape[0] // seq
    e = xp.shape[1]
    qbp, kp, kbp, vp, vtp, zp, lfp = _proj(xp, g, wb, "fox", (bf,), True, cache)
    qs, ks, vs, zs, lfs = _proj(xs, g, wb, "fox", (bf,), False)
    r3 = lambda a: a.reshape(b, seq, e)
    bias = _fox_bias(lfp.reshape(b, seq, nh))
    op = _attn_prompt(r3(qbp), r3(kbp), vtp, "fox", bias, 0.0, seq).reshape(b * seq, e)
    os_ = _decode_attn(qs, ks, vs, k_pool, v_pool, page_table, "fox", (lfs.T, lf_pool), 0.0, j)
    xp = _oproj(op, zp, xp, wo, "fox")
    xs = _oproj(os_, zs, xs, wo, "fox")
    return xp, xs, kp, vp, lfp, ks, vs, lfs


def _ssm_layer(xp, xs, h0_re, h0_im, g, w_in, a_re, a_im, log_step, b_re, b_im, c_re, c_im, d, w_glu, w_out, seq):
    ng, p = a_re.shape
    ch, n = SSM_CH, SSM_CHUNK
    wb = w_in.astype(BF16)
    wg = w_glu.astype(BF16)
    wo = w_out.astype(BF16)
    b = xp.shape[0] // seq
    e = xp.shape[1]
    pw_re, pw_im, f_re, f_im = _ssm_discretize(a_re, a_im, log_step)
    t_tab, x_tab, f_tab, bb = _ssm_tables(pw_re, pw_im, f_re, f_im, b_re, b_im, c_re, c_im)
    dd = d.astype(F32)
    flat = lambda a: a[:, :p].reshape(1, ng * p)
    gs = LANES // ch
    ns = ng // gs
    t_slab = _slab_expand(t_tab, ns, gs, ch)
    e_slab = _slab_expand(x_tab, ns, gs, p)
    f_slab = jnp.swapaxes(_slab_expand(f_tab, ns, gs, p), 1, 2).reshape(ns, 2, gs * p, n * LANES)
    d_slab = jnp.broadcast_to(dd.reshape(ns, 1, gs * ch), (ns, n, gs * ch)).reshape(ns, 1, n * LANES)
    up, zp = _proj(xp, g, wb, "ssm", (), True)
    nchunk = b * seq // n
    cb = min(SSM_CHUNK_ROWS, nchunk)
    s_re, s_im = _ssm_local(up, e_slab, cb)
    h_re, h_im, fin_re, fin_im = _ssm_scan(s_re, s_im, flat(pw_re[n]), flat(pw_im[n]), b)
    gp = _ssm_out(up, t_slab, f_slab, d_slab, h_re, h_im, cb)
    xp = _oproj(gp, zp, xp, wo, "ssm", wg=wg)
    us, zs = _proj(xs, g, wb, "ssm", (), False)
    bd = xs.shape[0]
    gs, ns_re, ns_im = _ssm_step(us, h0_re.reshape(bd, ng * p).astype(F32), h0_im.reshape(bd, ng * p).astype(F32),
                                 flat(pw_re[1]), flat(pw_im[1]), bb, c_re.astype(F32), c_im.astype(F32), dd)
    xs = _oproj(gs, zs, xs, wo, "ssm", wg=wg)
    return xp, xs, fin_re.reshape(b, ng, p), fin_im.reshape(b, ng, p), ns_re.reshape(bd, ng, p), ns_im.reshape(bd, ng, p)


def kernel(x_prompt, x_sample, cache_diff_k, cache_diff_v, cache_fox_k, cache_fox_v, cache_fox_logf, state_ssm_re, state_ssm_im, page_table, norm_g, final_norm_g, diff_w_in, diff_lambda, diff_subln_g, diff_w_out, fox_w_in, fox_b_f, fox_w_out, ssm_w_in, ssm_a_re, ssm_a_im, ssm_log_step, ssm_b_re, ssm_b_im, ssm_c_re, ssm_c_im, ssm_d, ssm_w_glu, ssm_w_out):
    b, seq, dm = x_prompt.shape
    bd = x_sample.shape[0]
    depth = norm_g.shape[0]
    xp = x_prompt.reshape(b * seq, dm)
    xs = x_sample.reshape(bd, dm)
    rows = lambda c: c.reshape(c.shape[0], c.shape[1], c.shape[2] * c.shape[3], c.shape[4])
    cols = lambda c: jnp.transpose(c, (0, 1, 3, 4, 2)).reshape(c.shape[0], c.shape[1], c.shape[3] * c.shape[4],
                                                                c.shape[2])
    dk_pool, dv_pool = rows(cache_diff_k), rows(cache_diff_v)
    fk_pool, fv_pool = cols(cache_fox_k), cols(cache_fox_v)
    fl_pool = jnp.transpose(cache_fox_logf, (0, 1, 3, 2))
    acc = {name: [] for name in ("dks", "dvs", "flp", "fks", "fvs", "fls", "srp", "sip", "srs", "sis")}
    ha, hb = cache_diff_k.shape[3], cache_fox_k.shape[3]
    diff_kv = fox_kv = None
    for i in range(depth):
        kind, j = i % N_MIXERS, i // N_MIXERS
        final_g = final_norm_g if i == depth - 1 else None
        if kind == 0:
            xp, xs, kp, vp, ks, vs = _diff_layer(
                xp, xs, dk_pool, dv_pool, j, page_table, norm_g[i], diff_w_in[j], diff_lambda[j],
                diff_subln_g[j], diff_w_out[j], _diff_lambda_init(i), final_g, seq,
                (j, diff_w_in.shape[0], diff_kv, b))
            diff_kv = (kp, vp)
            acc["dks"].append(ks); acc["dvs"].append(vs)
        elif kind == 1:
            xp, xs, kp, vp, lfp, ks, vs, lfs = _fox_layer(
                xp, xs, fk_pool, fv_pool, fl_pool, j, page_table, norm_g[i], fox_w_in[j], fox_b_f[j],
                fox_w_out[j], seq, (j, fox_w_in.shape[0], fox_kv, b))
            fox_kv = (kp, vp)
            acc["flp"].append(lfp)
            acc["fks"].append(ks); acc["fvs"].append(vs); acc["fls"].append(lfs)
        else:
            xp, xs, hrp, hip, hrs, his = _ssm_layer(
                xp, xs, state_ssm_re[j], state_ssm_im[j], norm_g[i], ssm_w_in[j], ssm_a_re[j], ssm_a_im[j],
                ssm_log_step[j], ssm_b_re[j], ssm_b_im[j], ssm_c_re[j], ssm_c_im[j], ssm_d[j], ssm_w_glu[j],
                ssm_w_out[j], seq)
            acc["srp"].append(hrp); acc["sip"].append(hip); acc["srs"].append(hrs); acc["sis"].append(his)
    stack = lambda name, shape: jnp.stack(acc[name]).reshape((len(acc[name]),) + shape)
    diff_rows = lambda a: a.reshape(a.shape[0], b, seq, ha, dm // ha)
    fox_rows = lambda a: a.reshape(a.shape[0], b, hb, dm // hb, seq).transpose(0, 1, 4, 2, 3)
    return (xp.reshape(b, seq, dm), xs.reshape(bd, 1, dm),
            diff_rows(diff_kv[0]), diff_rows(diff_kv[1]),
            stack("dks", (bd, 1, ha, dm // ha)), stack("dvs", (bd, 1, ha, dm // ha)),
            fox_rows(fox_kv[0]), fox_rows(fox_kv[1]),
            stack("flp", (b, seq, hb)),
            stack("fks", (bd, 1, hb, dm // hb)), stack("fvs", (bd, 1, hb, dm // hb)),
            stack("fls", (bd, 1, hb)),
            jnp.stack(acc["srp"]), jnp.stack(acc["sip"]), jnp.stack(acc["srs"]), jnp.stack(acc["sis"]))
```

```python
import functools
import math

import numpy as np

import jax
import jax.numpy as jnp
from jax import lax
from jax.experimental import pallas as pl
from jax.experimental.pallas import tpu as pltpu

F32 = jnp.float32
BF16 = jnp.bfloat16
HIGHEST = lax.Precision.HIGHEST

LANES = 128
SUBLANES = 8
D_MODEL = 1024
PAGE_SIZE = 128
N_MIXERS = 3
D_HEAD = 64
SSM_CH = 16
SSM_P = 64
SSM_CHUNK = 16
NORM_EPS = 1e-6
SUBLN_EPS = 1e-5
NEG_INF = -1e30
ROPE_THETA = 10000.0
LOG2E = math.log2(math.e)
QK_SCALE = D_HEAD ** -0.5 * LOG2E
BF16_ROWS = 16
VT_ROWS = LANES + BF16_ROWS
VMEM_LIMIT = 56 * 1024 * 1024

PROJ_ROWS = 256
OPROJ_ROWS = 512
CUMSUM_ROWS = 256
SSM_CHUNK_ROWS = 256
ATTN_TILE = 2048
ATTN_QSUB = 256
ATTN_AHEAD = 6

NT_DIMS = (((1,), (1,)), ((), ()))


def _params(*sem):
    return pltpu.CompilerParams(dimension_semantics=sem, vmem_limit_bytes=VMEM_LIMIT)


def _diff_lambda_init(layer_idx):
    return 0.8 - 0.6 * math.exp(-0.3 * layer_idx)


def _rms(x, g, eps):
    return x * lax.rsqrt(jnp.mean(x * x, axis=-1, keepdims=True) + eps) * g


def _silu(z):
    return z * jax.nn.sigmoid(z)


def _diff_lambda(lv, lam_init):
    return (jnp.exp(jnp.sum(lv[0:1] * lv[1:2], axis=1, keepdims=True))
            - jnp.exp(jnp.sum(lv[2:3] * lv[3:4], axis=1, keepdims=True)) + lam_init)


def _rope(a, cos, sin_signed, first_half):
    swapped = jnp.where(first_half, pltpu.roll(a, 96, 1), pltpu.roll(a, 32, 1))
    return a * cos + swapped * sin_signed


def _proj_body(*refs, mode, prompt):
    e = D_MODEL
    refs = list(refs)
    x_ref, g_ref, w_ref = refs[:3]
    refs = refs[3:]
    xn = _rms(x_ref[...], g_ref[...], NORM_EPS).astype(BF16)

    def proj(lo, width=e):
        return jnp.dot(xn, w_ref[:, lo:lo + width], preferred_element_type=F32)

    if mode == "ssm":
        u_ref, z_ref = refs
        u_ref[...] = proj(0)
        z_ref[...] = proj(e)
        return
    if mode == "diff":
        cos_ref, sin_ref = refs[:2]
        refs = refs[2:]
    else:
        bf_ref = refs.pop(0)
        lf_ref = refs.pop()
    if prompt:
        q_ref, k_ref, kb_ref, v_ref, vt_ref, z_ref = refs[-6:]
    else:
        q_ref, k_ref, v_ref, z_ref = refs
    q = proj(0) * QK_SCALE
    k = proj(e)
    if mode == "diff":
        cos = cos_ref[...]
        sin = sin_ref[...]
        lane = lax.broadcasted_iota(jnp.int32, (1, LANES), 1)
        first_half = (lane % D_HEAD) < (D_HEAD // 2)
        blocks = [slice(h * LANES, (h + 1) * LANES) for h in range(e // LANES)]
        q = jnp.concatenate([_rope(q[:, sl], cos, sin, first_half) for sl in blocks], axis=1)
        k = jnp.concatenate([_rope(k[:, sl], cos, sin, first_half) for sl in blocks], axis=1)
    q_ref[...] = q.astype(q_ref.dtype)
    v = proj(2 * e)
    if not prompt:
        k_ref[...] = k
        v_ref[...] = v
    else:
        kb_ref[...] = k.astype(BF16)
        ones = jnp.ones((BF16_ROWS, v.shape[0]), BF16)
        for h in range(e // LANES):
            sl = slice(h * LANES, (h + 1) * LANES)
            v_t = v[:, sl].T
            vt_ref[h * VT_ROWS:h * VT_ROWS + LANES, :] = v_t.astype(BF16)
            vt_ref[h * VT_ROWS + LANES:(h + 1) * VT_ROWS, :] = ones
            if mode == "fox":
                k_ref[sl, :] = k[:, sl].T
                v_ref[sl, :] = v_t
            else:
                k_ref[:, h, :] = k[:, sl]
                v_ref[:, h, :] = v[:, sl]
    z_ref[...] = proj(3 * e)
    if mode == "fox":
        n_heads = lf_ref.shape[-1]
        f = proj(4 * e, LANES)[:, :n_heads] + bf_ref[...]
        lf_ref[...] = jax.nn.log_sigmoid(f)


def _proj(x, g, w, mode, extras, prompt, cache=None):
    t, d = x.shape
    n = w.shape[1]
    tm = min(t, PROJ_ROWS)
    row = lambda i: (i, 0)
    fixed = lambda i: (0, 0)
    args = [x, g.reshape(1, d), w, *extras]
    in_specs = [pl.BlockSpec((tm, d), row), pl.BlockSpec((1, d), fixed), pl.BlockSpec((d, n), fixed)]
    f32_out = (jax.ShapeDtypeStruct((t, d), F32), pl.BlockSpec((tm, d), row))
    bf_out = (jax.ShapeDtypeStruct((t, d), BF16), pl.BlockSpec((tm, d), row))
    aliases = {}
    if mode == "ssm":
        outs = [f32_out, f32_out]
    else:
        if mode == "diff":
            nblk = extras[0].shape[0] // tm
            tab = pl.BlockSpec((tm, LANES), lambda i: (i % nblk, 0))
            in_specs += [tab, tab]
        else:
            in_specs += [pl.BlockSpec(extras[0].shape, fixed)]
        if prompt:
            slot, n_slots, prev, batch = cache
            vt_rows = d // LANES * VT_ROWS
            vt_out = (jax.ShapeDtypeStruct((vt_rows, t), BF16), pl.BlockSpec((vt_rows, tm), lambda i: (0, i)))
            if mode == "fox":
                nb = t // batch // tm
                kv_out = (jax.ShapeDtypeStruct((n_slots, batch, d, t // batch), F32),
                          pl.BlockSpec((None, None, d, tm), lambda i: (slot, i // nb, 0, i % nb)))
            else:
                kv_out = (jax.ShapeDtypeStruct((n_slots, t, d // LANES, LANES), F32),
                          pl.BlockSpec((None, tm, d // LANES, LANES), lambda i: (slot, i, 0, 0)))
            outs = [bf_out, kv_out, bf_out, kv_out, vt_out, f32_out]
            if prev is not None:
                aliases = {len(args): 1, len(args) + 1: 3}
                args += list(prev)
                in_specs += [pl.BlockSpec(memory_space=pl.ANY)] * 2
        else:
            outs = [f32_out] * 4
        if mode == "fox":
            nh = extras[0].shape[1]
            outs.append((jax.ShapeDtypeStruct((t, nh), F32), pl.BlockSpec((tm, nh), row)))
    return pl.pallas_call(
        functools.partial(_proj_body, mode=mode, prompt=prompt),
        grid=(t // tm,),
        in_specs=in_specs,
        out_specs=[o[1] for o in outs],
        out_shape=[o[0] for o in outs],
        input_output_aliases=aliases,
        compiler_params=_params("parallel"),
        name=f"proj_{mode}",
    )(*args)


def _rope_tables(pos):
    half = D_HEAD // 2
    inv = ROPE_THETA ** (-jnp.arange(half, dtype=F32) / half)
    ang = pos.astype(F32)[:, None] * inv[None, :]
    cos, sin = jnp.cos(ang), jnp.sin(ang)
    cos = jnp.concatenate([cos, cos, cos, cos], axis=1)
    sin = jnp.concatenate([-sin, sin, -sin, sin], axis=1)
    return cos, sin


def _colmax(st, group=4 * SUBLANES):
    r, c = st.shape
    return jnp.max(jnp.max(st.reshape(r // group, group, c), axis=0), axis=0, keepdims=True)


def _attn_body(qi_ref, kj_ref, *refs, kind, lam_init, tq, qsub):
    if kind == "diff":
        q_ref, k_ref, vt_ref, lam_ref, o_ref, m_sc, acc_sc = refs
    else:
        q_ref, qx0_ref, qx1_ref, k_ref, kx_ref, vt_ref, o_ref, m_sc, acc_sc = refs
    n = pl.program_id(2)
    qi = qi_ref[n]
    kj = kj_ref[n]
    lane = lax.broadcasted_iota(jnp.int32, (1, LANES), 1)
    low = lane < D_HEAD

    @pl.when(kj == 0)
    def _init():
        m_sc[...] = jnp.full(m_sc.shape, NEG_INF, F32)
        acc_sc[...] = jnp.zeros(acc_sc.shape, F32)

    def step(diagonal):
        q = q_ref[...]
        k = k_ref[...]
        if kind == "fox":
            k = jnp.concatenate([k, kx_ref[...]], axis=1)
        vt = vt_ref[...]
        qs = []
        for s in range(2):
            qm = jnp.where(low if s == 0 else jnp.logical_not(low), q, jnp.zeros_like(q))
            if kind == "fox":
                qm = jnp.concatenate([qm, (qx0_ref if s == 0 else qx1_ref)[...]], axis=1)
            qs.append(qm)
        chains = [(s, j) for s in range(2) for j in range(tq // qsub)]

        def scores(s, j):
            cs = slice(j * qsub, (j + 1) * qsub)
            rows = (j + 1) * qsub if diagonal else tq
            st = lax.dot_general(k[:rows], qs[s][cs], NT_DIMS, preferred_element_type=F32)
            if diagonal:
                kpos = lax.broadcasted_iota(jnp.int32, st.shape, 0)
                qpos = j * qsub + lax.broadcasted_iota(jnp.int32, st.shape, 1)
                st = jnp.where(kpos <= qpos, st, NEG_INF)
            return s, cs, rows, st

        def softmax_values(s, cs, rows, st):
            m_prev = m_sc[s, :, cs]
            m_new = jnp.maximum(m_prev, _colmax(st))
            alpha = jnp.exp2(m_prev - m_new)
            p = jnp.exp2(st - m_new).astype(BF16)
            acc_sc[s, :, cs] = alpha * acc_sc[s, :, cs] + jnp.dot(vt[:, :rows], p, preferred_element_type=F32)
            m_sc[s, :, cs] = m_new

        pending = [scores(*c) for c in chains[:ATTN_AHEAD]]
        for c in chains[ATTN_AHEAD:]:
            softmax_values(*pending.pop(0))
            pending.append(scores(*c))
        for done in pending:
            softmax_values(*done)

    @pl.when(kj < qi)
    def _full():
        step(False)

    @pl.when(kj == qi)
    def _diag():
        step(True)
        o = [acc_sc[s, :LANES, :] / acc_sc[s, LANES:LANES + 1, :] for s in range(2)]
        if kind == "diff":
            w = o[0] - _diff_lambda(lam_ref[...], lam_init) * o[1]
        else:
            row = lax.broadcasted_iota(jnp.int32, o[0].shape, 0)
            w = jnp.where(row < D_HEAD, o[0], o[1])
        o_ref[...] = w.T


def _attn_prompt(qb, kb, vt, kind, extras, lam_init, seq):
    b, l, e = qb.shape
    tq, qsub = min(ATTN_TILE, seq), min(ATTN_QSUB, seq)
    nblk = l // tq
    steps = [(i, j) for i in range(nblk) for j in range(i + 1)]
    qi, kj = (jnp.asarray([s[c] for s in steps], jnp.int32) for c in range(2))
    q_spec = pl.BlockSpec((None, tq, LANES), lambda bb, h, n, qi, kj: (bb, qi[n], h))
    k_spec = pl.BlockSpec((None, tq, LANES), lambda bb, h, n, qi, kj: (bb, kj[n], h))
    vt_spec = pl.BlockSpec((VT_ROWS, tq), lambda bb, h, n, qi, kj: (h, bb * nblk + kj[n]))
    if kind == "diff":
        (lam,) = extras
        args = [qb, kb, vt, lam]
        in_specs = [q_spec, k_spec, vt_spec, pl.BlockSpec(lam.shape, lambda bb, h, n, qi, kj: (0, 0))]
    else:
        kx, qx0, qx1 = extras
        args = [qb, qx0, qx1, kb, kx, vt]
        in_specs = [q_spec, q_spec, q_spec, k_spec, k_spec, vt_spec]
    return pl.pallas_call(
        functools.partial(_attn_body, kind=kind, lam_init=lam_init, tq=tq, qsub=qsub),
        grid_spec=pltpu.PrefetchScalarGridSpec(
            num_scalar_prefetch=2,
            grid=(b, e // LANES, len(steps)),
            in_specs=in_specs,
            out_specs=q_spec,
            scratch_shapes=[pltpu.VMEM((2, 1, tq), F32), pltpu.VMEM((2, VT_ROWS, tq), F32)]),
        out_shape=jax.ShapeDtypeStruct((b, l, e), F32),
        compiler_params=_params("parallel", "parallel", "arbitrary"),
        name=f"attn_{kind}",
    )(qi, kj, *args)


def _fox_bias_body(lf_ref, sel_ref, const_ref, kx_ref, qx0_ref, qx1_ref, carry_sc):
    @pl.when(pl.program_id(1) == 0)
    def _():
        carry_sc[...] = jnp.zeros(carry_sc.shape, F32)

    lf = lf_ref[...]
    t = lf.shape[0]
    r = lax.broadcasted_iota(jnp.int32, (t, t), 0)
    c = lax.broadcasted_iota(jnp.int32, (t, t), 1)
    tri = (c <= r).astype(F32)
    cs = jnp.dot(tri, lf, preferred_element_type=F32, precision=HIGHEST) + carry_sc[...]
    carry_sc[...] = cs[t - 1:t, :]
    c2 = cs * LOG2E
    hi = c2.astype(BF16)
    rest = c2 - hi.astype(F32)
    mid = rest.astype(BF16)
    lo = (rest - mid.astype(F32)).astype(BF16)
    for o, out_ref in enumerate((kx_ref, qx0_ref, qx1_ref)):
        placed = const_ref[o]
        for piece, val in enumerate((hi, mid, lo)):
            placed = placed + jnp.dot(val, sel_ref[o, piece], preferred_element_type=F32)
        out_ref[...] = placed.astype(BF16)


def _fox_bias_layout(nh):
    sel = np.zeros((3, 3, nh, nh // 2 * LANES), np.float32)
    const = np.zeros((3, 1, nh // 2 * LANES), np.float32)
    for head in range(nh):
        base, s = head // 2 * LANES, head % 2
        for piece in range(3):
            sel[0, piece, head, base + 3 * s + piece] = 1.0
            sel[1 + s, piece, head, base + 6 + piece] = 1.0
            const[0, 0, base + 6 + piece] = 1.0
            const[1 + s, 0, base + 3 * s + piece] = -1.0
    return jnp.asarray(sel, BF16), jnp.asarray(const, F32)


def _fox_bias(logf):
    b, l, nh = logf.shape
    tm = min(l, CUMSUM_ROWS)
    sel, const = _fox_bias_layout(nh)
    width = sel.shape[-1]
    out = (jax.ShapeDtypeStruct((b, l, width), BF16), pl.BlockSpec((None, tm, width), lambda bb, i: (bb, i, 0)))
    return pl.pallas_call(
        _fox_bias_body,
        grid=(b, l // tm),
        in_specs=[pl.BlockSpec((None, tm, nh), lambda bb, i: (bb, i, 0)),
                  pl.BlockSpec(sel.shape, lambda bb, i: (0, 0, 0, 0)),
                  pl.BlockSpec(const.shape, lambda bb, i: (0, 0, 0))],
        out_specs=[out[1]] * 3,
        out_shape=[out[0]] * 3,
        scratch_shapes=[pltpu.VMEM((1, nh), F32)],
        compiler_params=_params("parallel", "arbitrary"),
        name="fox_bias",
    )(logf, sel, const)


def _oproj_body(*refs, mode, lam_init, final):
    refs = list(refs)
    a_ref, z_ref, x_ref = refs[:3]
    refs = refs[3:]
    if mode == "ssm":
        wg_ref = refs.pop(0)
    w_ref = refs.pop(0)
    if mode == "diff":
        sg_ref = refs.pop(0)
    if final:
        fg_ref = refs.pop(0)
    out_ref = refs.pop(0)
    e = D_MODEL
    z = z_ref[...]
    if mode == "diff":
        a = a_ref[...]
        sg = sg_ref[...] * (1.0 - lam_init)
        a = jnp.concatenate(
            [_rms(a[:, h * LANES:(h + 1) * LANES], 1.0, SUBLN_EPS) * sg for h in range(e // LANES)], axis=1)
    elif mode == "ssm":
        gl = jnp.dot(a_ref[...].astype(BF16), wg_ref[...], preferred_element_type=F32)
        a = gl[:, :e] * jax.nn.sigmoid(gl[:, e:])
    else:
        a = a_ref[...]
    y = jnp.dot((a * _silu(z)).astype(BF16), w_ref[...], preferred_element_type=F32)
    xn = x_ref[...] + y
    if final:
        out_ref[...] = _rms(xn, fg_ref[...], NORM_EPS)
    else:
        out_ref[...] = xn


def _oproj(a, z, x, w, mode, wg=None, subln_g=None, lam_init=0.0, final_g=None):
    t, d = x.shape
    tm = min(t, OPROJ_ROWS)
    row = lambda i: (i, 0)
    fixed = lambda i: (0, 0)
    blk = pl.BlockSpec((tm, d), row)
    args = [a, z, x]
    in_specs = [blk, blk, blk]
    if mode == "ssm":
        args.append(wg)
        in_specs.append(pl.BlockSpec(wg.shape, fixed))
    args.append(w)
    in_specs.append(pl.BlockSpec(w.shape, fixed))
    if mode == "diff":
        args.append(subln_g.reshape(1, LANES))
        in_specs.append(pl.BlockSpec((1, LANES), fixed))
    if final_g is not None:
        args.append(final_g.reshape(1, d))
        in_specs.append(pl.BlockSpec((1, d), fixed))
    return pl.pallas_call(
        functools.partial(_oproj_body, mode=mode, lam_init=lam_init, final=final_g is not None),
        grid=(t // tm,),
        in_specs=in_specs,
        out_specs=blk,
        out_shape=jax.ShapeDtypeStruct((t, d), F32),
        compiler_params=_params("parallel"),
        name=f"oproj_{mode}",
    )(*args)


def _decode_diff_body(pt_ref, q_ref, kn_ref, vn_ref, lam_ref, *refs, lam_init, n_pages):
    del pt_ref
    k_refs = refs[:n_pages]
    v_refs = refs[n_pages:2 * n_pages]
    o_ref = refs[2 * n_pages]
    q8 = q_ref[...]
    nh = q8.shape[0]
    low = lax.broadcasted_iota(jnp.int32, q8.shape, 1) < D_HEAD
    q = jnp.concatenate([jnp.where(low, q8, 0.0), jnp.where(low, 0.0, q8)], axis=0)
    qb = q.astype(BF16)
    rows = k_refs[0].shape[0]
    own = (lax.broadcasted_iota(jnp.int32, (2 * nh, rows), 1) % nh
           == lax.broadcasted_iota(jnp.int32, (2 * nh, rows), 0) % nh)
    s = jnp.concatenate(
        [jnp.where(own, lax.dot_general(qb, k_refs[p][...].astype(BF16), NT_DIMS, preferred_element_type=F32),
                   NEG_INF) for p in range(n_pages)], axis=1)
    kn = kn_ref[...]
    vn = vn_ref[...]
    s_new = jnp.sum(q * jnp.concatenate([kn, kn], axis=0), axis=1, keepdims=True)
    m = jnp.maximum(jnp.max(s, axis=1, keepdims=True), s_new)
    p_past = jnp.exp2(s - m)
    p_new = jnp.exp2(s_new - m)
    l = jnp.sum(p_past, axis=1, keepdims=True) + p_new
    pb = p_past.astype(BF16)
    acc = p_new * jnp.concatenate([vn, vn], axis=0)
    for p in range(n_pages):
        acc = acc + jnp.dot(pb[:, p * rows:(p + 1) * rows], v_refs[p][...].astype(BF16),
                            preferred_element_type=F32)
    o = acc / l
    o_ref[...] = o[:nh] - _diff_lambda(lam_ref[...], lam_init) * o[nh:]


def _decode_fox_body(pt_ref, q_ref, kn_ref, vn_ref, lfn_ref, *refs, n_pages):
    del pt_ref
    kt_refs = refs[:n_pages]
    vt_refs = refs[n_pages:2 * n_pages]
    lf_refs = refs[2 * n_pages:3 * n_pages]
    o_ref = refs[3 * n_pages]
    e = q_ref.shape[-1]
    nh = e // D_HEAD
    b = pl.program_id(0)
    row = lax.broadcasted_iota(jnp.int32, (nh, e), 0)
    lane = lax.broadcasted_iota(jnp.int32, (nh, e), 1)
    own = (lane // D_HEAD) == row
    q = jnp.where(own, q_ref[...], 0.0)
    qb = q.astype(BF16)
    s = jnp.concatenate(
        [jnp.dot(qb, kt_refs[p][...].astype(BF16), preferred_element_type=F32) for p in range(n_pages)], axis=1)
    s_new = jnp.sum(q * kn_ref[...], axis=1, keepdims=True)
    page = lf_refs[0].shape[1]
    lft = jnp.concatenate([lf_refs[p][...] for p in range(n_pages)], axis=0)
    r = lax.broadcasted_iota(jnp.int32, (page, page), 0)
    c = lax.broadcasted_iota(jnp.int32, (page, page), 1)
    within = jnp.dot(lft, (r <= c).astype(F32), preferred_element_type=F32, precision=HIGHEST)
    offs = jnp.zeros((nh, 1), F32)
    c_past = []
    for p in range(n_pages):
        w = within[p * nh:(p + 1) * nh]
        c_past.append(w + offs)
        offs = offs + w[:, page - 1:page]
    c_past = jnp.concatenate(c_past, axis=1)
    lfn = lfn_ref[...]
    bcol = lax.broadcasted_iota(jnp.int32, lfn.shape, 1)
    c_new = offs + jnp.sum(jnp.where(bcol == b, lfn, 0.0), axis=1, keepdims=True)
    s = s + (c_new - c_past) * LOG2E
    m = jnp.maximum(jnp.max(s, axis=1, keepdims=True), s_new)
    p_past = jnp.exp2(s - m)
    p_new = jnp.exp2(s_new - m)
    l = jnp.sum(p_past, axis=1, keepdims=True) + p_new
    pb = p_past.astype(BF16)
    acc = p_new * vn_ref[...]
    for p in range(n_pages):
        acc = acc + lax.dot_general(pb[:, p * page:(p + 1) * page], vt_refs[p][...].astype(BF16), NT_DIMS,
                                    preferred_element_type=F32)
    o_ref[...] = jnp.sum(jnp.where(own, acc / l, 0.0), axis=0, keepdims=True)


def _decode_attn(q, k_new, v_new, k_pool, v_pool, page_table, kind, extras, lam_init, layer):
    bd, e = q.shape
    n_pages = page_table.shape[1]
    if kind == "diff":
        vec_shape = (bd, e // LANES, LANES)
    else:
        vec_shape = (bd, 1, e)
    vec = pl.BlockSpec((None,) + vec_shape[1:], lambda b, pt: (b, 0, 0))
    in_specs = [vec, vec, vec]
    args = [q.reshape(vec_shape), k_new.reshape(vec_shape), v_new.reshape(vec_shape)]
    pools = [k_pool, v_pool]
    if kind == "diff":
        (lam,) = extras
        args.append(lam)
        in_specs.append(pl.BlockSpec(lam.shape, lambda b, pt: (0, 0)))
        body = functools.partial(_decode_diff_body, lam_init=lam_init, n_pages=n_pages)
    else:
        lf_new_t, lf_pool = extras
        args.append(lf_new_t)
        in_specs.append(pl.BlockSpec(lf_new_t.shape, lambda b, pt: (0, 0)))
        pools.append(lf_pool)
        body = functools.partial(_decode_fox_body, n_pages=n_pages)

    def page_spec(pool, p):
        return pl.BlockSpec((None, None) + pool.shape[2:], lambda b, pt: (layer, pt[b, p], 0, 0))

    for pool in pools:
        args += [pool] * n_pages
        in_specs += [page_spec(pool, p) for p in range(n_pages)]
    out = pl.pallas_call(
        body,
        grid_spec=pltpu.PrefetchScalarGridSpec(
            num_scalar_prefetch=1,
            grid=(bd,),
            in_specs=in_specs,
            out_specs=vec),
        out_shape=jax.ShapeDtypeStruct(vec_shape, F32),
        compiler_params=_params("parallel"),
        name=f"decode_{kind}",
    )(page_table, *args)
    return out.reshape(bd, e)


def _ssm_disc_body(are_ref, aim_ref, dt_ref, pr_ref, pi_ref, fre_ref, fim_ref):
    a_re = jnp.minimum(are_ref[...], -1e-4)
    a_im = aim_ref[...]
    dt = dt_ref[...]
    mag = jnp.exp(a_re * dt)
    ang = a_im * dt
    lb_re = mag * jnp.cos(ang)
    lb_im = mag * jnp.sin(ang)
    den = a_re * a_re + a_im * a_im
    num_re = lb_re - 1.0
    num_im = lb_im
    fre_ref[...] = (num_re * a_re + num_im * a_im) / den
    fim_ref[...] = (num_im * a_re - num_re * a_im) / den
    pr = jnp.ones_like(lb_re)
    pi = jnp.zeros_like(lb_im)
    for j in range(SSM_CHUNK + 1):
        pr_ref[j] = pr
        pi_ref[j] = pi
        pr, pi = pr * lb_re - pi * lb_im, pr * lb_im + pi * lb_re


def _twice(a):
    return jnp.concatenate([a, a], axis=-1).astype(F32)


def _ssm_discretize(a_re, a_im, log_step):
    g, p = a_re.shape
    dt = jnp.broadcast_to(jnp.exp(log_step.astype(F32))[:, None], (g, 2 * p))
    pw = jax.ShapeDtypeStruct((SSM_CHUNK + 1, g, 2 * p), F32)
    gp = jax.ShapeDtypeStruct((g, 2 * p), F32)
    return pl.pallas_call(_ssm_disc_body, out_shape=[pw, pw, gp, gp], name="ssm_discretize")(
        _twice(a_re), _twice(a_im), dt)


def _ssm_tables_body(pr_ref, pi_ref, fre_ref, fim_ref, br_ref, bi_ref, cr_ref, ci_ref,
                     t_ref, x_ref, f_ref, bb_ref):
    n = SSM_CHUNK
    ch = SSM_CH
    low = lax.broadcasted_iota(jnp.int32, (1, 2 * SSM_P), 1) < SSM_P
    fre = fre_ref[...]
    fim = fim_ref[...]
    br = br_ref[...]
    bi = bi_ref[...]
    bbr = fre * br - fim * bi
    bbi = fre * bi + fim * br
    bb_ref[...] = jnp.where(low, bbr, bbi)
    cr = cr_ref[...]
    ci = ci_ref[...]
    x_rows = []
    for j in range(n):
        pr = pr_ref[n - 1 - j]
        pi = pi_ref[n - 1 - j]
        x_rows.append(bbr * jnp.where(low, pr, pi) + bbi * jnp.where(low, -pi, pr))
    x_all = jnp.concatenate(x_rows, axis=0)
    x_ref[...] = x_all
    c_cat = jnp.where(low, cr, -ci)
    c_rep = jnp.concatenate([c_cat] * n, axis=0)
    k_rev = lax.dot_general(x_all, c_rep, NT_DIMS, preferred_element_type=F32, precision=HIGHEST)
    blk = lax.broadcasted_iota(jnp.int32, (n * ch, n * ch), 1) // ch
    toe = jnp.zeros((n * ch, n * ch), F32)
    for t in range(n):
        sh = ch * (n - 1 - t)
        shifted = k_rev if sh == 0 else jnp.concatenate([k_rev[sh:], jnp.zeros((sh, n * ch), F32)], axis=0)
        toe = jnp.where(blk == t, shifted, toe)
    t_ref[...] = toe
    f_rows = []
    for t in range(n):
        pr = pr_ref[t + 1]
        pi = pi_ref[t + 1]
        f_rows.append(cr * jnp.where(low, pr, -pi) + ci * jnp.where(low, -pi, -pr))
    f_ref[...] = jnp.concatenate(f_rows, axis=0)


def _ssm_tables(pw_re, pw_im, f_re, f_im, b_re, b_im, c_re, c_im):
    g, p2 = f_re.shape
    n, ch = SSM_CHUNK, SSM_CH
    nc = n * ch
    gmap = lambda i: (i, 0, 0)
    pw_spec = pl.BlockSpec((n + 1, None, 1, p2), lambda i: (0, i, 0, 0))
    row_spec = pl.BlockSpec((None, 1, p2), gmap)
    mat_spec = pl.BlockSpec((None, ch, p2), gmap)
    out = lambda rows, cols: (jax.ShapeDtypeStruct((g, rows, cols), F32), pl.BlockSpec((None, rows, cols), gmap))
    outs = [out(nc, nc), out(nc, p2), out(nc, p2), out(ch, p2)]
    return pl.pallas_call(
        _ssm_tables_body,
        grid=(g,),
        in_specs=[pw_spec, pw_spec, row_spec, row_spec, mat_spec, mat_spec, mat_spec, mat_spec],
        out_specs=[o[1] for o in outs],
        out_shape=[o[0] for o in outs],
        compiler_params=_params("parallel"),
        name="ssm_tables",
    )(pw_re.reshape(n + 1, g, 1, p2), pw_im.reshape(n + 1, g, 1, p2), f_re.reshape(g, 1, p2), f_im.reshape(g, 1, p2),
      _twice(jnp.swapaxes(b_re, 1, 2)), _twice(jnp.swapaxes(b_im, 1, 2)), _twice(c_re), _twice(c_im))


def _slab_expand(tab, ns, gs, minor):
    g, rows, cols = tab.shape
    n, ch = SSM_CHUNK, SSM_CH
    w = tab.reshape(ns, gs, n, ch, cols).transpose(0, 2, 1, 3, 4).reshape(ns, n * gs * ch, cols)
    x = cols // minor
    spread = np.zeros((x, minor, x, gs, minor), np.float32)
    for i in range(x):
        for m in range(minor):
            spread[i, m, i, :, m] = 1.0
    spread = jnp.asarray(spread.reshape(cols, x * gs * minor), BF16)
    row_g = (np.arange(n * gs * ch) // ch) % gs
    col_g = (np.arange(x * gs * minor) // minor) % gs
    keep = jnp.asarray(row_g[:, None] == col_g[None, :])
    z = jnp.einsum("arc,cd->ard", w.astype(BF16), spread, preferred_element_type=BF16)
    return jnp.where(keep[None], z, jnp.zeros_like(z))


def _chunk_rows(u_ref, cb):
    return jnp.concatenate([u_ref[pl.ds(t, cb, stride=SSM_CHUNK), :] for t in range(SSM_CHUNK)], axis=1)


def _ssm_local_body(u_ref, e_ref, sre_ref, sim_ref):
    cb = sre_ref.shape[0]
    s = jnp.dot(_chunk_rows(u_ref, cb).astype(BF16), e_ref[...], preferred_element_type=F32)
    half = s.shape[1] // 2
    sre_ref[...] = s[:, :half]
    sim_ref[...] = s[:, half:]


def _ssm_local(u, e_slab, cb):
    t, e = u.shape
    ns, kdim, width = e_slab.shape
    nchunk = t // SSM_CHUNK
    half = width // 2
    st = jax.ShapeDtypeStruct((nchunk, ns * half), F32)
    return pl.pallas_call(
        _ssm_local_body,
        grid=(ns, nchunk // cb),
        in_specs=[pl.BlockSpec((cb * SSM_CHUNK, LANES), lambda a, i: (i, a)),
                  pl.BlockSpec((None, kdim, width), lambda a, i: (a, 0, 0))],
        out_specs=[pl.BlockSpec((cb, half), lambda a, i: (i, a))] * 2,
        out_shape=[st, st],
        compiler_params=_params("parallel", "parallel"),
        name="ssm_local",
    )(u, e_slab)


def _ssm_scan_body(sre_ref, sim_ref, ar_ref, ai_ref, hre_ref, him_ref, fre_ref, fim_ref, *, per_seq):
    n_seq = sre_ref.shape[0] // per_seq
    ar = ar_ref[...]
    ai = ai_ref[...]
    zero = jnp.zeros_like(ar)

    def body(i, carry):
        new = []
        for q in range(n_seq):
            hr, hi = carry[2 * q], carry[2 * q + 1]
            row = q * per_seq + i
            hre_ref[pl.ds(row, 1), :] = hr
            him_ref[pl.ds(row, 1), :] = hi
            new.append(ar * hr - ai * hi + sre_ref[pl.ds(row, 1), :])
            new.append(ar * hi + ai * hr + sim_ref[pl.ds(row, 1), :])
        return tuple(new)

    fin = lax.fori_loop(0, per_seq, body, (zero,) * (2 * n_seq))
    for q in range(n_seq):
        fre_ref[pl.ds(q, 1), :] = fin[2 * q]
        fim_ref[pl.ds(q, 1), :] = fin[2 * q + 1]


def _ssm_scan(s_re, s_im, a_re, a_im, n_seq):
    rows, width = s_re.shape
    wblk = 1024
    full = jax.ShapeDtypeStruct((rows, width), F32)
    fin = jax.ShapeDtypeStruct((n_seq, width), F32)
    col = lambda i: (0, i)
    return pl.pallas_call(
        functools.partial(_ssm_scan_body, per_seq=rows // n_seq),
        grid=(width // wblk,),
        in_specs=[pl.BlockSpec((rows, wblk), col)] * 2 + [pl.BlockSpec((1, wblk), col)] * 2,
        out_specs=[pl.BlockSpec((rows, wblk), col)] * 2 + [pl.BlockSpec((n_seq, wblk), col)] * 2,
        out_shape=[full, full, fin, fin],
        compiler_params=_params("parallel"),
        name="ssm_scan",
    )(s_re, s_im, a_re, a_im)


def _ssm_out_body(u_ref, t_ref, f_ref, d_ref, hre_ref, him_ref, g_ref):
    cb = hre_ref.shape[0]
    u = _chunk_rows(u_ref, cb)
    y = jnp.dot(u.astype(BF16), t_ref[...], preferred_element_type=F32)
    y = y + jnp.dot(hre_ref[...].astype(BF16), f_ref[0], preferred_element_type=F32)
    y = y + jnp.dot(him_ref[...].astype(BF16), f_ref[1], preferred_element_type=F32)
    g = jax.nn.gelu(y + d_ref[...] * u)
    for t in range(SSM_CHUNK):
        g_ref[pl.ds(t, cb, stride=SSM_CHUNK), :] = g[:, t * LANES:(t + 1) * LANES]


def _ssm_out(u, t_slab, f_slab, d_slab, h_re, h_im, cb):
    t, e = u.shape
    ns, kdim, _ = t_slab.shape
    half = f_slab.shape[2]
    rows = cb * SSM_CHUNK
    return pl.pallas_call(
        _ssm_out_body,
        grid=(ns, t // rows),
        in_specs=[pl.BlockSpec((rows, LANES), lambda a, i: (i, a)),
                  pl.BlockSpec((None, kdim, kdim), lambda a, i: (a, 0, 0)),
                  pl.BlockSpec((None, 2, half, kdim), lambda a, i: (a, 0, 0, 0)),
                  pl.BlockSpec((None, 1, kdim), lambda a, i: (a, 0, 0)),
                  pl.BlockSpec((cb, half), lambda a, i: (i, a)),
                  pl.BlockSpec((cb, half), lambda a, i: (i, a))],
        out_specs=pl.BlockSpec((rows, LANES), lambda a, i: (i, a)),
        out_shape=jax.ShapeDtypeStruct((t, e), F32),
        compiler_params=_params("parallel", "parallel"),
        name="ssm_out",
    )(u, t_slab, f_slab, d_slab, h_re, h_im)


def _ssm_step_body(u_ref, hr_ref, hi_ref, lr_ref, li_ref, wb_ref, cr_ref, ci_ref, d_ref,
                   g_ref, nr_ref, ni_ref):
    u = u_ref[...]
    bu = jnp.dot(u.astype(BF16), wb_ref[...].astype(BF16), preferred_element_type=F32)
    half = bu.shape[1] // 2
    lr, li, hr, hi = lr_ref[...], li_ref[...], hr_ref[...], hi_ref[...]
    nr = lr * hr - li * hi + bu[:, :half]
    ni = lr * hi + li * hr + bu[:, half:]
    nr_ref[...] = nr
    ni_ref[...] = ni
    y = (jnp.dot(nr.astype(BF16), cr_ref[...].astype(BF16), preferred_element_type=F32)
         - jnp.dot(ni.astype(BF16), ci_ref[...].astype(BF16), preferred_element_type=F32)
         + d_ref[...] * u)
    g_ref[...] = jax.nn.gelu(y).astype(BF16)


def _ssm_step(u, h_re, h_im, lb_re, lb_im, bb, c_re, c_im, d):
    bd, e = u.shape
    g, ch, p2 = bb.shape
    p = p2 // 2
    gs = LANES // ch
    ns = g // gs
    eye = jnp.eye(gs, dtype=F32)
    wb = jnp.einsum("sgcrp,gh->sgcrhp", bb.reshape(ns, gs, ch, 2, p), eye).reshape(ns, LANES, 2 * gs * p)
    crb = jnp.einsum("sgcp,gh->shpgc", c_re.reshape(ns, gs, ch, p), eye).reshape(ns, gs * p, LANES)
    cib = jnp.einsum("sgcp,gh->shpgc", c_im.reshape(ns, gs, ch, p), eye).reshape(ns, gs * p, LANES)
    st = jax.ShapeDtypeStruct((bd, g * p), F32)
    col = lambda i: (0, i)
    slab = lambda i: (i, 0, 0)
    return pl.pallas_call(
        _ssm_step_body,
        grid=(ns,),
        in_specs=[pl.BlockSpec((bd, LANES), col),
                  pl.BlockSpec((bd, gs * p), col), pl.BlockSpec((bd, gs * p), col),
                  pl.BlockSpec((1, gs * p), col), pl.BlockSpec((1, gs * p), col),
                  pl.BlockSpec((None, LANES, 2 * gs * p), slab),
                  pl.BlockSpec((None, gs * p, LANES), slab), pl.BlockSpec((None, gs * p, LANES), slab),
                  pl.BlockSpec((1, LANES), col)],
        out_specs=[pl.BlockSpec((bd, LANES), col),
                   pl.BlockSpec((bd, gs * p), col), pl.BlockSpec((bd, gs * p), col)],
        out_shape=[jax.ShapeDtypeStruct((bd, e), BF16), st, st],
        compiler_params=_params("parallel"),
        name="ssm_step",
    )(u, h_re, h_im, lb_re, lb_im, wb, crb, cib, d.reshape(1, g * ch))


def _diff_layer(xp, xs, k_pool, v_pool, j, page_table, g, w_in, lam_vec, subln_g, w_out, lam_init, final_g, seq,
                cache):
    wb = w_in.astype(BF16)
    wo = w_out.astype(BF16)
    lam = lam_vec.astype(F32)
    b = xp.shape[0] // seq
    e = xp.shape[1]
    past = page_table.shape[1] * PAGE_SIZE
    qbp, kp, kbp, vp, vtp, zp = _proj(xp, g, wb, "diff", _rope_tables(jnp.arange(seq)), True, cache)
    qs, ks, vs, zs = _proj(xs, g, wb, "diff", _rope_tables(jnp.full((xs.shape[0],), past)), False)
    r3 = lambda a: a.reshape(b, seq, e)
    op = _attn_prompt(r3(qbp), r3(kbp), vtp, "diff", (lam,), lam_init, seq).reshape(b * seq, e)
    os_ = _decode_attn(qs, ks, vs, k_pool, v_pool, page_table, "diff", (lam,), lam_init, j)
    xp = _oproj(op, zp, xp, wo, "diff", subln_g=subln_g, lam_init=lam_init, final_g=final_g)
    xs = _oproj(os_, zs, xs, wo, "diff", subln_g=subln_g, lam_init=lam_init, final_g=final_g)
    return xp, xs, kp, vp, ks, vs


def _fox_layer(xp, xs, k_pool, v_pool, lf_pool, j, page_table, g, w_in, b_f, w_out, seq, cache):
    nh = b_f.shape[0]
    wb = jnp.pad(w_in, ((0, 0), (0, LANES - nh))).astype(BF16)
    wo = w_out.astype(BF16)
    bf = b_f.reshape(1, nh).astype(F32)
    b = xp.shape[0] // seq
    e = xp.shape[1]
    qbp, kp, kbp, vp, vtp, zp, lfp = _proj(xp, g, wb, "fox", (bf,), True, cache)
    qs, ks, vs, zs, lfs = _proj(xs, g, wb, "fox", (bf,), False)
    r3 = lambda a: a.reshape(b, seq, e)
    bias = _fox_bias(lfp.reshape(b, seq, nh))
    op = _attn_prompt(r3(qbp), r3(kbp), vtp, "fox", bias, 0.0, seq).reshape(b * seq, e)
    os_ = _decode_attn(qs, ks, vs, k_pool, v_pool, page_table, "fox", (lfs.T, lf_pool), 0.0, j)
    xp = _oproj(op, zp, xp, wo, "fox")
    xs = _oproj(os_, zs, xs, wo, "fox")
    return xp, xs, kp, vp, lfp, ks, vs, lfs


def _ssm_layer(xp, xs, h0_re, h0_im, g, w_in, a_re, a_im, log_step, b_re, b_im, c_re, c_im, d, w_glu, w_out, seq):
    ng, p = a_re.shape
    ch, n = SSM_CH, SSM_CHUNK
    wb = w_in.astype(BF16)
    wg = w_glu.astype(BF16)
    wo = w_out.astype(BF16)
    b = xp.shape[0] // seq
    e = xp.shape[1]
    pw_re, pw_im, f_re, f_im = _ssm_discretize(a_re, a_im, log_step)
    t_tab, x_tab, f_tab, bb = _ssm_tables(pw_re, pw_im, f_re, f_im, b_re, b_im, c_re, c_im)
    dd = d.astype(F32)
    flat = lambda a: a[:, :p].reshape(1, ng * p)
    gs = LANES // ch
    ns = ng // gs
    t_slab = _slab_expand(t_tab, ns, gs, ch)
    e_slab = _slab_expand(x_tab, ns, gs, p)
    f_slab = jnp.swapaxes(_slab_expand(f_tab, ns, gs, p), 1, 2).reshape(ns, 2, gs * p, n * LANES)
    d_slab = jnp.broadcast_to(dd.reshape(ns, 1, gs * ch), (ns, n, gs * ch)).reshape(ns, 1, n * LANES)
    up, zp = _proj(xp, g, wb, "ssm", (), True)
    nchunk = b * seq // n
    cb = min(SSM_CHUNK_ROWS, nchunk)
    s_re, s_im = _ssm_local(up, e_slab, cb)
    h_re, h_im, fin_re, fin_im = _ssm_scan(s_re, s_im, flat(pw_re[n]), flat(pw_im[n]), b)
    gp = _ssm_out(up, t_slab, f_slab, d_slab, h_re, h_im, cb)
    xp = _oproj(gp, zp, xp, wo, "ssm", wg=wg)
    us, zs = _proj(xs, g, wb, "ssm", (), False)
    bd = xs.shape[0]
    gs, ns_re, ns_im = _ssm_step(us, h0_re.reshape(bd, ng * p).astype(F32), h0_im.reshape(bd, ng * p).astype(F32),
                                 flat(pw_re[1]), flat(pw_im[1]), bb, c_re.astype(F32), c_im.astype(F32), dd)
    xs = _oproj(gs, zs, xs, wo, "ssm", wg=wg)
    return xp, xs, fin_re.reshape(b, ng, p), fin_im.reshape(b, ng, p), ns_re.reshape(bd, ng, p), ns_im.reshape(bd, ng, p)


def kernel(x_prompt, x_sample, cache_diff_k, cache_diff_v, cache_fox_k, cache_fox_v, cache_fox_logf, state_ssm_re, state_ssm_im, page_table, norm_g, final_norm_g, diff_w_in, diff_lambda, diff_subln_g, diff_w_out, fox_w_in, fox_b_f, fox_w_out, ssm_w_in, ssm_a_re, ssm_a_im, ssm_log_step, ssm_b_re, ssm_b_im, ssm_c_re, ssm_c_im, ssm_d, ssm_w_glu, ssm_w_out):
    b, seq, dm = x_prompt.shape
    bd = x_sample.shape[0]
    depth = norm_g.shape[0]
    xp = x_prompt.reshape(b * seq, dm)
    xs = x_sample.reshape(bd, dm)
    rows = lambda c: c.reshape(c.shape[0], c.shape[1], c.shape[2] * c.shape[3], c.shape[4])
    cols = lambda c: jnp.transpose(c, (0, 1, 3, 4, 2)).reshape(c.shape[0], c.shape[1], c.shape[3] * c.shape[4],
                                                                c.shape[2])
    dk_pool, dv_pool = rows(cache_diff_k), rows(cache_diff_v)
    fk_pool, fv_pool = cols(cache_fox_k), cols(cache_fox_v)
    fl_pool = jnp.transpose(cache_fox_logf, (0, 1, 3, 2))
    acc = {name: [] for name in ("dks", "dvs", "flp", "fks", "fvs", "fls", "srp", "sip", "srs", "sis")}
    ha, hb = cache_diff_k.shape[3], cache_fox_k.shape[3]
    diff_kv = fox_kv = None
    for i in range(depth):
        kind, j = i % N_MIXERS, i // N_MIXERS
        final_g = final_norm_g if i == depth - 1 else None
        if kind == 0:
            xp, xs, kp, vp, ks, vs = _diff_layer(
                xp, xs, dk_pool, dv_pool, j, page_table, norm_g[i], diff_w_in[j], diff_lambda[j],
                diff_subln_g[j], diff_w_out[j], _diff_lambda_init(i), final_g, seq,
                (j, diff_w_in.shape[0], diff_kv, b))
            diff_kv = (kp, vp)
            acc["dks"].append(ks); acc["dvs"].append(vs)
        elif kind == 1:
            xp, xs, kp, vp, lfp, ks, vs, lfs = _fox_layer(
                xp, xs, fk_pool, fv_pool, fl_pool, j, page_table, norm_g[i], fox_w_in[j], fox_b_f[j],
                fox_w_out[j], seq, (j, fox_w_in.shape[0], fox_kv, b))
            fox_kv = (kp, vp)
            acc["flp"].append(lfp)
            acc["fks"].append(ks); acc["fvs"].append(vs); acc["fls"].append(lfs)
        else:
            xp, xs, hrp, hip, hrs, his = _ssm_layer(
                xp, xs, state_ssm_re[j], state_ssm_im[j], norm_g[i], ssm_w_in[j], ssm_a_re[j], ssm_a_im[j],
                ssm_log_step[j], ssm_b_re[j], ssm_b_im[j], ssm_c_re[j], ssm_c_im[j], ssm_d[j], ssm_w_glu[j],
                ssm_w_out[j], seq)
            acc["srp"].append(hrp); acc["sip"].append(hip); acc["srs"].append(hrs); acc["sis"].append(his)
    stack = lambda name, shape: jnp.stack(acc[name]).reshape((len(acc[name]),) + shape)
    diff_rows = lambda a: a.reshape(a.shape[0], b, seq, ha, dm // ha)
    fox_rows = lambda a: a.reshape(a.shape[0], b, hb, dm // hb, seq).transpose(0, 1, 4, 2, 3)
    return (xp.reshape(b, seq, dm), xs.reshape(bd, 1, dm),
            diff_rows(diff_kv[0]), diff_rows(diff_kv[1]),
            stack("dks", (bd, 1, ha, dm // ha)), stack("dvs", (bd, 1, ha, dm // ha)),
            fox_rows(fox_kv[0]), fox_rows(fox_kv[1]),
            stack("flp", (b, seq, hb)),
            stack("fks", (bd, 1, hb, dm // hb)), stack("fvs", (bd, 1, hb, dm // hb)),
            stack("fls", (bd, 1, hb)),
            jnp.stack(acc["srp"]), jnp.stack(acc["sip"]), jnp.stack(acc["srs"]), jnp.stack(acc["sis"]))
```

```python
import functools
import math

import numpy as np

import jax
import jax.numpy as jnp
from jax import lax
from jax.experimental import pallas as pl
from jax.experimental.pallas import tpu as pltpu

F32 = jnp.float32
BF16 = jnp.bfloat16
HIGHEST = lax.Precision.HIGHEST

LANES = 128
SUBLANES = 8
D_MODEL = 1024
PAGE_SIZE = 128
N_MIXERS = 3
D_HEAD = 64
SSM_CH = 16
SSM_P = 64
SSM_CHUNK = 16
NORM_EPS = 1e-6
SUBLN_EPS = 1e-5
NEG_INF = -1e30
ROPE_THETA = 10000.0
LOG2E = math.log2(math.e)
QK_SCALE = D_HEAD ** -0.5 * LOG2E
BIAS_PIECES = 3
BF16_ROWS = 16
VT_ROWS = LANES + BF16_ROWS
VMEM_LIMIT = 56 * 1024 * 1024

PROJ_ROWS = 256
OPROJ_ROWS = 512
CUMSUM_ROWS = 256
SSM_CHUNK_ROWS = 256
ATTN_TILE = 2048
ATTN_QSUB = 256
ATTN_AHEAD = 6

NT_DIMS = (((1,), (1,)), ((), ()))


def _params(*sem):
    return pltpu.CompilerParams(dimension_semantics=sem, vmem_limit_bytes=VMEM_LIMIT)


def _diff_lambda_init(layer_idx):
    return 0.8 - 0.6 * math.exp(-0.3 * layer_idx)


def _rms(x, g, eps):
    return x * lax.rsqrt(jnp.mean(x * x, axis=-1, keepdims=True) + eps) * g


def _silu(z):
    return z * jax.nn.sigmoid(z)


def _diff_lambda(lv, lam_init):
    return (jnp.exp(jnp.sum(lv[0:1] * lv[1:2], axis=1, keepdims=True))
            - jnp.exp(jnp.sum(lv[2:3] * lv[3:4], axis=1, keepdims=True)) + lam_init)


def _rope(a, cos, sin_signed, first_half):
    swapped = jnp.where(first_half, pltpu.roll(a, 96, 1), pltpu.roll(a, 32, 1))
    return a * cos + swapped * sin_signed


def _proj_body(*refs, mode, prompt):
    e = D_MODEL
    refs = list(refs)
    x_ref, g_ref, w_ref = refs[:3]
    refs = refs[3:]
    xn = _rms(x_ref[...], g_ref[...], NORM_EPS).astype(BF16)

    def proj(lo, width=e):
        return jnp.dot(xn, w_ref[:, lo:lo + width], preferred_element_type=F32)

    if mode == "ssm":
        u_ref, z_ref = refs
        u_ref[...] = proj(0)
        z_ref[...] = proj(e)
        return
    if mode == "diff":
        cos_ref, sin_ref = refs[:2]
        refs = refs[2:]
    else:
        bf_ref = refs.pop(0)
        lf_ref = refs.pop()
    if prompt:
        q_ref, k_ref, kb_ref, v_ref, vt_ref, z_ref = refs[-6:]
    else:
        q_ref, k_ref, v_ref, z_ref = refs
    q = proj(0) * QK_SCALE
    k = proj(e)
    if mode == "diff":
        cos = cos_ref[...]
        sin = sin_ref[...]
        lane = lax.broadcasted_iota(jnp.int32, (1, LANES), 1)
        first_half = (lane % D_HEAD) < (D_HEAD // 2)
        blocks = [slice(h * LANES, (h + 1) * LANES) for h in range(e // LANES)]
        q = jnp.concatenate([_rope(q[:, sl], cos, sin, first_half) for sl in blocks], axis=1)
        k = jnp.concatenate([_rope(k[:, sl], cos, sin, first_half) for sl in blocks], axis=1)
    q_ref[...] = q.astype(q_ref.dtype)
    v = proj(2 * e)
    if not prompt:
        k_ref[...] = k
        v_ref[...] = v
    else:
        kb_ref[...] = k.astype(BF16)
        ones = jnp.ones((BF16_ROWS, v.shape[0]), BF16)
        for h in range(e // LANES):
            sl = slice(h * LANES, (h + 1) * LANES)
            v_t = v[:, sl].T
            vt_ref[h * VT_ROWS:h * VT_ROWS + LANES, :] = v_t.astype(BF16)
            vt_ref[h * VT_ROWS + LANES:(h + 1) * VT_ROWS, :] = ones
            if mode == "fox":
                k_ref[sl, :] = k[:, sl].T
                v_ref[sl, :] = v_t
            else:
                k_ref[:, h, :] = k[:, sl]
                v_ref[:, h, :] = v[:, sl]
    z_ref[...] = proj(3 * e)
    if mode == "fox":
        n_heads = lf_ref.shape[-1]
        f = proj(4 * e, LANES)[:, :n_heads] + bf_ref[...]
        lf_ref[...] = jax.nn.log_sigmoid(f)


def _proj(x, g, w, mode, extras, prompt, cache=None):
    t, d = x.shape
    n = w.shape[1]
    tm = min(t, PROJ_ROWS)
    row = lambda i: (i, 0)
    fixed = lambda i: (0, 0)
    args = [x, g.reshape(1, d), w, *extras]
    in_specs = [pl.BlockSpec((tm, d), row), pl.BlockSpec((1, d), fixed), pl.BlockSpec((d, n), fixed)]
    f32_out = (jax.ShapeDtypeStruct((t, d), F32), pl.BlockSpec((tm, d), row))
    bf_out = (jax.ShapeDtypeStruct((t, d), BF16), pl.BlockSpec((tm, d), row))
    aliases = {}
    if mode == "ssm":
        outs = [f32_out, f32_out]
    else:
        if mode == "diff":
            nblk = extras[0].shape[0] // tm
            tab = pl.BlockSpec((tm, LANES), lambda i: (i % nblk, 0))
            in_specs += [tab, tab]
        else:
            in_specs += [pl.BlockSpec(extras[0].shape, fixed)]
        if prompt:
            slot, n_slots, prev, batch = cache
            vt_rows = d // LANES * VT_ROWS
            vt_out = (jax.ShapeDtypeStruct((vt_rows, t), BF16), pl.BlockSpec((vt_rows, tm), lambda i: (0, i)))
            if mode == "fox":
                nb = t // batch // tm
                kv_out = (jax.ShapeDtypeStruct((n_slots, batch, d, t // batch), F32),
                          pl.BlockSpec((None, None, d, tm), lambda i: (slot, i // nb, 0, i % nb)))
            else:
                kv_out = (jax.ShapeDtypeStruct((n_slots, t, d // LANES, LANES), F32),
                          pl.BlockSpec((None, tm, d // LANES, LANES), lambda i: (slot, i, 0, 0)))
            outs = [bf_out, kv_out, bf_out, kv_out, vt_out, f32_out]
            if prev is not None:
                aliases = {len(args): 1, len(args) + 1: 3}
                args += list(prev)
                in_specs += [pl.BlockSpec(memory_space=pl.ANY)] * 2
        else:
            outs = [f32_out] * 4
        if mode == "fox":
            nh = extras[0].shape[1]
            outs.append((jax.ShapeDtypeStruct((t, nh), F32), pl.BlockSpec((tm, nh), row)))
    return pl.pallas_call(
        functools.partial(_proj_body, mode=mode, prompt=prompt),
        grid=(t // tm,),
        in_specs=in_specs,
        out_specs=[o[1] for o in outs],
        out_shape=[o[0] for o in outs],
        input_output_aliases=aliases,
        compiler_params=_params("parallel"),
        name=f"proj_{mode}",
    )(*args)


def _rope_tables(pos):
    half = D_HEAD // 2
    inv = ROPE_THETA ** (-jnp.arange(half, dtype=F32) / half)
    ang = pos.astype(F32)[:, None] * inv[None, :]
    cos, sin = jnp.cos(ang), jnp.sin(ang)
    cos = jnp.concatenate([cos, cos, cos, cos], axis=1)
    sin = jnp.concatenate([-sin, sin, -sin, sin], axis=1)
    return cos, sin


def _colmax(st, group=4 * SUBLANES):
    r, c = st.shape
    return jnp.max(jnp.max(st.reshape(r // group, group, c), axis=0), axis=0, keepdims=True)


def _attn_body(qi_ref, kj_ref, *refs, kind, lam_init, tq, qsub):
    if kind == "diff":
        q_ref, k_ref, vt_ref, lam_ref, o_ref, m_sc, acc_sc = refs
    else:
        q_ref, qx_ref, k_ref, kx_ref, vt_ref, o_ref, m_sc, acc_sc, qx_sc = refs
    n = pl.program_id(2)
    qi = qi_ref[n]
    kj = kj_ref[n]
    lane = lax.broadcasted_iota(jnp.int32, (1, LANES), 1)
    low = lane < D_HEAD

    @pl.when(kj == 0)
    def _init():
        m_sc[...] = jnp.full(m_sc.shape, NEG_INF, F32)
        acc_sc[...] = jnp.zeros(acc_sc.shape, F32)
        if kind == "fox":
            w = BIAS_PIECES
            for s in range(2):
                cq = pltpu.roll(qx_ref[...].astype(F32), (2 - s) * w, 1)
                own = (lane >= s * w) & (lane < (s + 1) * w)
                ones = (lane >= 2 * w) & (lane < 3 * w)
                qx_sc[s] = jnp.where(ones, cq, jnp.where(own, -1.0, 0.0)).astype(BF16)

    def step(diagonal):
        q = q_ref[...]
        k = k_ref[...]
        if kind == "fox":
            k = jnp.concatenate([k, kx_ref[...]], axis=1)
        vt = vt_ref[...]
        qs = []
        for s in range(2):
            qm = jnp.where(low if s == 0 else jnp.logical_not(low), q, jnp.zeros_like(q))
            if kind == "fox":
                qm = jnp.concatenate([qm, qx_sc[s]], axis=1)
            qs.append(qm)
        chains = [(s, j) for s in range(2) for j in range(tq // qsub)]

        def scores(s, j):
            cs = slice(j * qsub, (j + 1) * qsub)
            rows = (j + 1) * qsub if diagonal else tq
            st = lax.dot_general(k[:rows], qs[s][cs], NT_DIMS, preferred_element_type=F32)
            if diagonal:
                kpos = lax.broadcasted_iota(jnp.int32, st.shape, 0)
                qpos = j * qsub + lax.broadcasted_iota(jnp.int32, st.shape, 1)
                st = jnp.where(kpos <= qpos, st, NEG_INF)
            return s, cs, rows, st

        def softmax_values(s, cs, rows, st):
            m_prev = m_sc[s, :, cs]
            m_new = jnp.maximum(m_prev, _colmax(st))
            alpha = jnp.exp2(m_prev - m_new)
            p = jnp.exp2(st - m_new).astype(BF16)
            acc_sc[s, :, cs] = alpha * acc_sc[s, :, cs] + jnp.dot(vt[:, :rows], p, preferred_element_type=F32)
            m_sc[s, :, cs] = m_new

        pending = [scores(*c) for c in chains[:ATTN_AHEAD]]
        for c in chains[ATTN_AHEAD:]:
            softmax_values(*pending.pop(0))
            pending.append(scores(*c))
        for done in pending:
            softmax_values(*done)

    @pl.when(kj < qi)
    def _full():
        step(False)

    @pl.when(kj == qi)
    def _diag():
        step(True)
        o = [acc_sc[s, :LANES, :] / acc_sc[s, LANES:LANES + 1, :] for s in range(2)]
        if kind == "diff":
            w = o[0] - _diff_lambda(lam_ref[...], lam_init) * o[1]
        else:
            row = lax.broadcasted_iota(jnp.int32, o[0].shape, 0)
            w = jnp.where(row < D_HEAD, o[0], o[1])
        o_ref[...] = w.T


def _attn_prompt(qb, kb, vt, kind, extras, lam_init, seq):
    b, l, e = qb.shape
    tq, qsub = min(ATTN_TILE, seq), min(ATTN_QSUB, seq)
    nblk = l // tq
    steps = [(i, j) for i in range(nblk) for j in range(i + 1)]
    qi, kj = (jnp.asarray([s[c] for s in steps], jnp.int32) for c in range(2))
    q_spec = pl.BlockSpec((None, tq, LANES), lambda bb, h, n, qi, kj: (bb, qi[n], h))
    k_spec = pl.BlockSpec((None, tq, LANES), lambda bb, h, n, qi, kj: (bb, kj[n], h))
    vt_spec = pl.BlockSpec((VT_ROWS, tq), lambda bb, h, n, qi, kj: (h, bb * nblk + kj[n]))
    scratch = [pltpu.VMEM((2, 1, tq), F32), pltpu.VMEM((2, VT_ROWS, tq), F32)]
    if kind == "diff":
        (lam,) = extras
        args = [qb, kb, vt, lam]
        in_specs = [q_spec, k_spec, vt_spec, pl.BlockSpec(lam.shape, lambda bb, h, n, qi, kj: (0, 0))]
    else:
        (cx,) = extras
        args = [qb, cx, kb, cx, vt]
        in_specs = [q_spec, q_spec, k_spec, k_spec, vt_spec]
        scratch.append(pltpu.VMEM((2, tq, LANES), BF16))
    return pl.pallas_call(
        functools.partial(_attn_body, kind=kind, lam_init=lam_init, tq=tq, qsub=qsub),
        grid_spec=pltpu.PrefetchScalarGridSpec(
            num_scalar_prefetch=2,
            grid=(b, e // LANES, len(steps)),
            in_specs=in_specs,
            out_specs=q_spec,
            scratch_shapes=scratch),
        out_shape=jax.ShapeDtypeStruct((b, l, e), F32),
        compiler_params=_params("parallel", "parallel", "arbitrary"),
        name=f"attn_{kind}",
    )(qi, kj, *args)


def _fox_bias_body(lf_ref, sel_ref, const_ref, cx_ref, carry_sc):
    @pl.when(pl.program_id(1) == 0)
    def _():
        carry_sc[...] = jnp.zeros(carry_sc.shape, F32)

    lf = lf_ref[...]
    t = lf.shape[0]
    r = lax.broadcasted_iota(jnp.int32, (t, t), 0)
    c = lax.broadcasted_iota(jnp.int32, (t, t), 1)
    tri = (c <= r).astype(F32)
    cs = jnp.dot(tri, lf, preferred_element_type=F32, precision=HIGHEST) + carry_sc[...]
    carry_sc[...] = cs[t - 1:t, :]
    c2 = cs * LOG2E
    hi = c2.astype(BF16)
    rest = c2 - hi.astype(F32)
    mid = rest.astype(BF16)
    lo = (rest - mid.astype(F32)).astype(BF16)
    placed = const_ref[...]
    for piece, val in enumerate((hi, mid, lo)):
        placed = placed + jnp.dot(val, sel_ref[piece], preferred_element_type=F32)
    cx_ref[...] = placed.astype(BF16)


def _fox_bias_layout(nh):
    sel = np.zeros((3, nh, nh // 2 * LANES), np.float32)
    const = np.zeros((1, nh // 2 * LANES), np.float32)
    for head in range(nh):
        base, s = head // 2 * LANES, head % 2
        for piece in range(3):
            sel[piece, head, base + BIAS_PIECES * s + piece] = 1.0
            const[0, base + 2 * BIAS_PIECES + piece] = 1.0
    return jnp.asarray(sel, BF16), jnp.asarray(const, F32)


def _fox_bias(logf):
    b, l, nh = logf.shape
    tm = min(l, CUMSUM_ROWS)
    sel, const = _fox_bias_layout(nh)
    width = sel.shape[-1]
    return pl.pallas_call(
        _fox_bias_body,
        grid=(b, l // tm),
        in_specs=[pl.BlockSpec((None, tm, nh), lambda bb, i: (bb, i, 0)),
                  pl.BlockSpec(sel.shape, lambda bb, i: (0, 0, 0)),
                  pl.BlockSpec(const.shape, lambda bb, i: (0, 0))],
        out_specs=pl.BlockSpec((None, tm, width), lambda bb, i: (bb, i, 0)),
        out_shape=jax.ShapeDtypeStruct((b, l, width), BF16),
        scratch_shapes=[pltpu.VMEM((1, nh), F32)],
        compiler_params=_params("parallel", "arbitrary"),
        name="fox_bias",
    )(logf, sel, const)


def _oproj_body(*refs, mode, lam_init, final):
    refs = list(refs)
    a_ref, z_ref, x_ref = refs[:3]
    refs = refs[3:]
    if mode == "ssm":
        wg_ref = refs.pop(0)
    w_ref = refs.pop(0)
    if mode == "diff":
        sg_ref = refs.pop(0)
    if final:
        fg_ref = refs.pop(0)
    out_ref = refs.pop(0)
    e = D_MODEL
    z = z_ref[...]
    if mode == "diff":
        a = a_ref[...]
        sg = sg_ref[...] * (1.0 - lam_init)
        a = jnp.concatenate(
            [_rms(a[:, h * LANES:(h + 1) * LANES], 1.0, SUBLN_EPS) * sg for h in range(e // LANES)], axis=1)
    elif mode == "ssm":
        gl = jnp.dot(a_ref[...].astype(BF16), wg_ref[...], preferred_element_type=F32)
        a = gl[:, :e] * jax.nn.sigmoid(gl[:, e:])
    else:
        a = a_ref[...]
    y = jnp.dot((a * _silu(z)).astype(BF16), w_ref[...], preferred_element_type=F32)
    xn = x_ref[...] + y
    if final:
        out_ref[...] = _rms(xn, fg_ref[...], NORM_EPS)
    else:
        out_ref[...] = xn


def _oproj(a, z, x, w, mode, wg=None, subln_g=None, lam_init=0.0, final_g=None):
    t, d = x.shape
    tm = min(t, OPROJ_ROWS)
    row = lambda i: (i, 0)
    fixed = lambda i: (0, 0)
    blk = pl.BlockSpec((tm, d), row)
    args = [a, z, x]
    in_specs = [blk, blk, blk]
    if mode == "ssm":
        args.append(wg)
        in_specs.append(pl.BlockSpec(wg.shape, fixed))
    args.append(w)
    in_specs.append(pl.BlockSpec(w.shape, fixed))
    if mode == "diff":
        args.append(subln_g.reshape(1, LANES))
        in_specs.append(pl.BlockSpec((1, LANES), fixed))
    if final_g is not None:
        args.append(final_g.reshape(1, d))
        in_specs.append(pl.BlockSpec((1, d), fixed))
    return pl.pallas_call(
        functools.partial(_oproj_body, mode=mode, lam_init=lam_init, final=final_g is not None),
        grid=(t // tm,),
        in_specs=in_specs,
        out_specs=blk,
        out_shape=jax.ShapeDtypeStruct((t, d), F32),
        compiler_params=_params("parallel"),
        name=f"oproj_{mode}",
    )(*args)


def _decode_diff_body(pt_ref, q_ref, kn_ref, vn_ref, lam_ref, *refs, lam_init, n_pages):
    del pt_ref
    k_refs = refs[:n_pages]
    v_refs = refs[n_pages:2 * n_pages]
    o_ref = refs[2 * n_pages]
    q8 = q_ref[...]
    nh = q8.shape[0]
    low = lax.broadcasted_iota(jnp.int32, q8.shape, 1) < D_HEAD
    q = jnp.concatenate([jnp.where(low, q8, 0.0), jnp.where(low, 0.0, q8)], axis=0)
    qb = q.astype(BF16)
    rows = k_refs[0].shape[0]
    own = (lax.broadcasted_iota(jnp.int32, (2 * nh, rows), 1) % nh
           == lax.broadcasted_iota(jnp.int32, (2 * nh, rows), 0) % nh)
    s = jnp.concatenate(
        [jnp.where(own, lax.dot_general(qb, k_refs[p][...].astype(BF16), NT_DIMS, preferred_element_type=F32),
                   NEG_INF) for p in range(n_pages)], axis=1)
    kn = kn_ref[...]
    vn = vn_ref[...]
    s_new = jnp.sum(q * jnp.concatenate([kn, kn], axis=0), axis=1, keepdims=True)
    m = jnp.maximum(jnp.max(s, axis=1, keepdims=True), s_new)
    p_past = jnp.exp2(s - m)
    p_new = jnp.exp2(s_new - m)
    l = jnp.sum(p_past, axis=1, keepdims=True) + p_new
    pb = p_past.astype(BF16)
    acc = p_new * jnp.concatenate([vn, vn], axis=0)
    for p in range(n_pages):
        acc = acc + jnp.dot(pb[:, p * rows:(p + 1) * rows], v_refs[p][...].astype(BF16),
                            preferred_element_type=F32)
    o = acc / l
    o_ref[...] = o[:nh] - _diff_lambda(lam_ref[...], lam_init) * o[nh:]


def _decode_fox_body(pt_ref, q_ref, kn_ref, vn_ref, lfn_ref, *refs, n_pages):
    del pt_ref
    kt_refs = refs[:n_pages]
    vt_refs = refs[n_pages:2 * n_pages]
    lf_refs = refs[2 * n_pages:3 * n_pages]
    o_ref = refs[3 * n_pages]
    e = q_ref.shape[-1]
    nh = e // D_HEAD
    b = pl.program_id(0)
    row = lax.broadcasted_iota(jnp.int32, (nh, e), 0)
    lane = lax.broadcasted_iota(jnp.int32, (nh, e), 1)
    own = (lane // D_HEAD) == row
    q = jnp.where(own, q_ref[...], 0.0)
    qb = q.astype(BF16)
    s = jnp.concatenate(
        [jnp.dot(qb, kt_refs[p][...].astype(BF16), preferred_element_type=F32) for p in range(n_pages)], axis=1)
    s_new = jnp.sum(q * kn_ref[...], axis=1, keepdims=True)
    page = lf_refs[0].shape[1]
    lft = jnp.concatenate([lf_refs[p][...] for p in range(n_pages)], axis=0)
    r = lax.broadcasted_iota(jnp.int32, (page, page), 0)
    c = lax.broadcasted_iota(jnp.int32, (page, page), 1)
    within = jnp.dot(lft, (r <= c).astype(F32), preferred_element_type=F32, precision=HIGHEST)
    offs = jnp.zeros((nh, 1), F32)
    c_past = []
    for p in range(n_pages):
        w = within[p * nh:(p + 1) * nh]
        c_past.append(w + offs)
        offs = offs + w[:, page - 1:page]
    c_past = jnp.concatenate(c_past, axis=1)
    lfn = lfn_ref[...]
    bcol = lax.broadcasted_iota(jnp.int32, lfn.shape, 1)
    c_new = offs + jnp.sum(jnp.where(bcol == b, lfn, 0.0), axis=1, keepdims=True)
    s = s + (c_new - c_past) * LOG2E
    m = jnp.maximum(jnp.max(s, axis=1, keepdims=True), s_new)
    p_past = jnp.exp2(s - m)
    p_new = jnp.exp2(s_new - m)
    l = jnp.sum(p_past, axis=1, keepdims=True) + p_new
    pb = p_past.astype(BF16)
    acc = p_new * vn_ref[...]
    for p in range(n_pages):
        acc = acc + lax.dot_general(pb[:, p * page:(p + 1) * page], vt_refs[p][...].astype(BF16), NT_DIMS,
                                    preferred_element_type=F32)
    o_ref[...] = jnp.sum(jnp.where(own, acc / l, 0.0), axis=0, keepdims=True)


def _decode_attn(q, k_new, v_new, k_pool, v_pool, page_table, kind, extras, lam_init, layer):
    bd, e = q.shape
    n_pages = page_table.shape[1]
    if kind == "diff":
        vec_shape = (bd, e // LANES, LANES)
    else:
        vec_shape = (bd, 1, e)
    vec = pl.BlockSpec((None,) + vec_shape[1:], lambda b, pt: (b, 0, 0))
    in_specs = [vec, vec, vec]
    args = [q.reshape(vec_shape), k_new.reshape(vec_shape), v_new.reshape(vec_shape)]
    pools = [k_pool, v_pool]
    if kind == "diff":
        (lam,) = extras
        args.append(lam)
        in_specs.append(pl.BlockSpec(lam.shape, lambda b, pt: (0, 0)))
        body = functools.partial(_decode_diff_body, lam_init=lam_init, n_pages=n_pages)
    else:
        lf_new_t, lf_pool = extras
        args.append(lf_new_t)
        in_specs.append(pl.BlockSpec(lf_new_t.shape, lambda b, pt: (0, 0)))
        pools.append(lf_pool)
        body = functools.partial(_decode_fox_body, n_pages=n_pages)

    def page_spec(pool, p):
        return pl.BlockSpec((None, None) + pool.shape[2:], lambda b, pt: (layer, pt[b, p], 0, 0))

    for pool in pools:
        args += [pool] * n_pages
        in_specs += [page_spec(pool, p) for p in range(n_pages)]
    out = pl.pallas_call(
        body,
        grid_spec=pltpu.PrefetchScalarGridSpec(
            num_scalar_prefetch=1,
            grid=(bd,),
            in_specs=in_specs,
            out_specs=vec),
        out_shape=jax.ShapeDtypeStruct(vec_shape, F32),
        compiler_params=_params("parallel"),
        name=f"decode_{kind}",
    )(page_table, *args)
    return out.reshape(bd, e)


def _ssm_disc_body(are_ref, aim_ref, dt_ref, pr_ref, pi_ref, fre_ref, fim_ref):
    a_re = jnp.minimum(are_ref[...], -1e-4)
    a_im = aim_ref[...]
    dt = dt_ref[...]
    mag = jnp.exp(a_re * dt)
    ang = a_im * dt
    lb_re = mag * jnp.cos(ang)
    lb_im = mag * jnp.sin(ang)
    den = a_re * a_re + a_im * a_im
    num_re = lb_re - 1.0
    num_im = lb_im
    fre_ref[...] = (num_re * a_re + num_im * a_im) / den
    fim_ref[...] = (num_im * a_re - num_re * a_im) / den
    pr = jnp.ones_like(lb_re)
    pi = jnp.zeros_like(lb_im)
    for j in range(SSM_CHUNK + 1):
        pr_ref[j] = pr
        pi_ref[j] = pi
        pr, pi = pr * lb_re - pi * lb_im, pr * lb_im + pi * lb_re


def _twice(a):
    return jnp.concatenate([a, a], axis=-1).astype(F32)


def _ssm_discretize(a_re, a_im, log_step):
    g, p = a_re.shape
    dt = jnp.broadcast_to(jnp.exp(log_step.astype(F32))[:, None], (g, 2 * p))
    pw = jax.ShapeDtypeStruct((SSM_CHUNK + 1, g, 2 * p), F32)
    gp = jax.ShapeDtypeStruct((g, 2 * p), F32)
    return pl.pallas_call(_ssm_disc_body, out_shape=[pw, pw, gp, gp], name="ssm_discretize")(
        _twice(a_re), _twice(a_im), dt)


def _ssm_tables_body(pr_ref, pi_ref, fre_ref, fim_ref, br_ref, bi_ref, cr_ref, ci_ref,
                     t_ref, x_ref, f_ref, bb_ref):
    n = SSM_CHUNK
    ch = SSM_CH
    low = lax.broadcasted_iota(jnp.int32, (1, 2 * SSM_P), 1) < SSM_P
    fre = fre_ref[...]
    fim = fim_ref[...]
    br = br_ref[...]
    bi = bi_ref[...]
    bbr = fre * br - fim * bi
    bbi = fre * bi + fim * br
    bb_ref[...] = jnp.where(low, bbr, bbi)
    cr = cr_ref[...]
    ci = ci_ref[...]
    x_rows = []
    for j in range(n):
        pr = pr_ref[n - 1 - j]
        pi = pi_ref[n - 1 - j]
        x_rows.append(bbr * jnp.where(low, pr, pi) + bbi * jnp.where(low, -pi, pr))
    x_all = jnp.concatenate(x_rows, axis=0)
    x_ref[...] = x_all
    c_cat = jnp.where(low, cr, -ci)
    c_rep = jnp.concatenate([c_cat] * n, axis=0)
    k_rev = lax.dot_general(x_all, c_rep, NT_DIMS, preferred_element_type=F32, precision=HIGHEST)
    blk = lax.broadcasted_iota(jnp.int32, (n * ch, n * ch), 1) // ch
    toe = jnp.zeros((n * ch, n * ch), F32)
    for t in range(n):
        sh = ch * (n - 1 - t)
        shifted = k_rev if sh == 0 else jnp.concatenate([k_rev[sh:], jnp.zeros((sh, n * ch), F32)], axis=0)
        toe = jnp.where(blk == t, shifted, toe)
    t_ref[...] = toe
    f_rows = []
    for t in range(n):
        pr = pr_ref[t + 1]
        pi = pi_ref[t + 1]
        f_rows.append(cr * jnp.where(low, pr, -pi) + ci * jnp.where(low, -pi, -pr))
    f_ref[...] = jnp.concatenate(f_rows, axis=0)


def _ssm_tables(pw_re, pw_im, f_re, f_im, b_re, b_im, c_re, c_im):
    g, p2 = f_re.shape
    n, ch = SSM_CHUNK, SSM_CH
    nc = n * ch
    gmap = lambda i: (i, 0, 0)
    pw_spec = pl.BlockSpec((n + 1, None, 1, p2), lambda i: (0, i, 0, 0))
    row_spec = pl.BlockSpec((None, 1, p2), gmap)
    mat_spec = pl.BlockSpec((None, ch, p2), gmap)
    out = lambda rows, cols: (jax.ShapeDtypeStruct((g, rows, cols), F32), pl.BlockSpec((None, rows, cols), gmap))
    outs = [out(nc, nc), out(nc, p2), out(nc, p2), out(ch, p2)]
    return pl.pallas_call(
        _ssm_tables_body,
        grid=(g,),
        in_specs=[pw_spec, pw_spec, row_spec, row_spec, mat_spec, mat_spec, mat_spec, mat_spec],
        out_specs=[o[1] for o in outs],
        out_shape=[o[0] for o in outs],
        compiler_params=_params("parallel"),
        name="ssm_tables",
    )(pw_re.reshape(n + 1, g, 1, p2), pw_im.reshape(n + 1, g, 1, p2), f_re.reshape(g, 1, p2), f_im.reshape(g, 1, p2),
      _twice(jnp.swapaxes(b_re, 1, 2)), _twice(jnp.swapaxes(b_im, 1, 2)), _twice(c_re), _twice(c_im))


def _slab_expand(tab, ns, gs, minor):
    g, rows, cols = tab.shape
    n, ch = SSM_CHUNK, SSM_CH
    w = tab.reshape(ns, gs, n, ch, cols).transpose(0, 2, 1, 3, 4).reshape(ns, n * gs * ch, cols)
    x = cols // minor
    spread = np.zeros((x, minor, x, gs, minor), np.float32)
    for i in range(x):
        for m in range(minor):
            spread[i, m, i, :, m] = 1.0
    spread = jnp.asarray(spread.reshape(cols, x * gs * minor), BF16)
    row_g = (np.arange(n * gs * ch) // ch) % gs
    col_g = (np.arange(x * gs * minor) // minor) % gs
    keep = jnp.asarray(row_g[:, None] == col_g[None, :])
    z = jnp.einsum("arc,cd->ard", w.astype(BF16), spread, preferred_element_type=BF16)
    return jnp.where(keep[None], z, jnp.zeros_like(z))


def _chunk_rows(u_ref, cb):
    return jnp.concatenate([u_ref[pl.ds(t, cb, stride=SSM_CHUNK), :] for t in range(SSM_CHUNK)], axis=1)


def _ssm_local_body(u_ref, e_ref, sre_ref, sim_ref):
    cb = sre_ref.shape[0]
    s = jnp.dot(_chunk_rows(u_ref, cb).astype(BF16), e_ref[...], preferred_element_type=F32)
    half = s.shape[1] // 2
    sre_ref[...] = s[:, :half]
    sim_ref[...] = s[:, half:]


def _ssm_local(u, e_slab, cb):
    t, e = u.shape
    ns, kdim, width = e_slab.shape
    nchunk = t // SSM_CHUNK
    half = width // 2
    st = jax.ShapeDtypeStruct((nchunk, ns * half), F32)
    return pl.pallas_call(
        _ssm_local_body,
        grid=(ns, nchunk // cb),
        in_specs=[pl.BlockSpec((cb * SSM_CHUNK, LANES), lambda a, i: (i, a)),
                  pl.BlockSpec((None, kdim, width), lambda a, i: (a, 0, 0))],
        out_specs=[pl.BlockSpec((cb, half), lambda a, i: (i, a))] * 2,
        out_shape=[st, st],
        compiler_params=_params("parallel", "parallel"),
        name="ssm_local",
    )(u, e_slab)


def _ssm_scan_body(sre_ref, sim_ref, ar_ref, ai_ref, hre_ref, him_ref, fre_ref, fim_ref, *, per_seq):
    n_seq = sre_ref.shape[0] // per_seq
    ar = ar_ref[...]
    ai = ai_ref[...]
    zero = jnp.zeros_like(ar)

    def body(i, carry):
        new = []
        for q in range(n_seq):
            hr, hi = carry[2 * q], carry[2 * q + 1]
            row = q * per_seq + i
            hre_ref[pl.ds(row, 1), :] = hr
            him_ref[pl.ds(row, 1), :] = hi
            new.append(ar * hr - ai * hi + sre_ref[pl.ds(row, 1), :])
            new.append(ar * hi + ai * hr + sim_ref[pl.ds(row, 1), :])
        return tuple(new)

    fin = lax.fori_loop(0, per_seq, body, (zero,) * (2 * n_seq))
    for q in range(n_seq):
        fre_ref[pl.ds(q, 1), :] = fin[2 * q]
        fim_ref[pl.ds(q, 1), :] = fin[2 * q + 1]


def _ssm_scan(s_re, s_im, a_re, a_im, n_seq):
    rows, width = s_re.shape
    wblk = 1024
    full = jax.ShapeDtypeStruct((rows, width), F32)
    fin = jax.ShapeDtypeStruct((n_seq, width), F32)
    col = lambda i: (0, i)
    return pl.pallas_call(
        functools.partial(_ssm_scan_body, per_seq=rows // n_seq),
        grid=(width // wblk,),
        in_specs=[pl.BlockSpec((rows, wblk), col)] * 2 + [pl.BlockSpec((1, wblk), col)] * 2,
        out_specs=[pl.BlockSpec((rows, wblk), col)] * 2 + [pl.BlockSpec((n_seq, wblk), col)] * 2,
        out_shape=[full, full, fin, fin],
        compiler_params=_params("parallel"),
        name="ssm_scan",
    )(s_re, s_im, a_re, a_im)


def _ssm_out_body(u_ref, t_ref, f_ref, d_ref, hre_ref, him_ref, g_ref):
    cb = hre_ref.shape[0]
    u = _chunk_rows(u_ref, cb)
    y = jnp.dot(u.astype(BF16), t_ref[...], preferred_element_type=F32)
    y = y + jnp.dot(hre_ref[...].astype(BF16), f_ref[0], preferred_element_type=F32)
    y = y + jnp.dot(him_ref[...].astype(BF16), f_ref[1], preferred_element_type=F32)
    g = jax.nn.gelu(y + d_ref[...] * u)
    for t in range(SSM_CHUNK):
        g_ref[pl.ds(t, cb, stride=SSM_CHUNK), :] = g[:, t * LANES:(t + 1) * LANES]


def _ssm_out(u, t_slab, f_slab, d_slab, h_re, h_im, cb):
    t, e = u.shape
    ns, kdim, _ = t_slab.shape
    half = f_slab.shape[2]
    rows = cb * SSM_CHUNK
    return pl.pallas_call(
        _ssm_out_body,
        grid=(ns, t // rows),
        in_specs=[pl.BlockSpec((rows, LANES), lambda a, i: (i, a)),
                  pl.BlockSpec((None, kdim, kdim), lambda a, i: (a, 0, 0)),
                  pl.BlockSpec((None, 2, half, kdim), lambda a, i: (a, 0, 0, 0)),
                  pl.BlockSpec((None, 1, kdim), lambda a, i: (a, 0, 0)),
                  pl.BlockSpec((cb, half), lambda a, i: (i, a)),
                  pl.BlockSpec((cb, half), lambda a, i: (i, a))],
        out_specs=pl.BlockSpec((rows, LANES), lambda a, i: (i, a)),
        out_shape=jax.ShapeDtypeStruct((t, e), F32),
        compiler_params=_params("parallel", "parallel"),
        name="ssm_out",
    )(u, t_slab, f_slab, d_slab, h_re, h_im)


def _ssm_step_body(u_ref, hr_ref, hi_ref, lr_ref, li_ref, wb_ref, cr_ref, ci_ref, d_ref,
                   g_ref, nr_ref, ni_ref):
    u = u_ref[...]
    bu = jnp.dot(u.astype(BF16), wb_ref[...].astype(BF16), preferred_element_type=F32)
    half = bu.shape[1] // 2
    lr, li, hr, hi = lr_ref[...], li_ref[...], hr_ref[...], hi_ref[...]
    nr = lr * hr - li * hi + bu[:, :half]
    ni = lr * hi + li * hr + bu[:, half:]
    nr_ref[...] = nr
    ni_ref[...] = ni
    y = (jnp.dot(nr.astype(BF16), cr_ref[...].astype(BF16), preferred_element_type=F32)
         - jnp.dot(ni.astype(BF16), ci_ref[...].astype(BF16), preferred_element_type=F32)
         + d_ref[...] * u)
    g_ref[...] = jax.nn.gelu(y).astype(BF16)


def _ssm_step(u, h_re, h_im, lb_re, lb_im, bb, c_re, c_im, d):
    bd, e = u.shape
    g, ch, p2 = bb.shape
    p = p2 // 2
    gs = LANES // ch
    ns = g // gs
    eye = jnp.eye(gs, dtype=F32)
    wb = jnp.einsum("sgcrp,gh->sgcrhp", bb.reshape(ns, gs, ch, 2, p), eye).reshape(ns, LANES, 2 * gs * p)
    crb = jnp.einsum("sgcp,gh->shpgc", c_re.reshape(ns, gs, ch, p), eye).reshape(ns, gs * p, LANES)
    cib = jnp.einsum("sgcp,gh->shpgc", c_im.reshape(ns, gs, ch, p), eye).reshape(ns, gs * p, LANES)
    st = jax.ShapeDtypeStruct((bd, g * p), F32)
    col = lambda i: (0, i)
    slab = lambda i: (i, 0, 0)
    return pl.pallas_call(
        _ssm_step_body,
        grid=(ns,),
        in_specs=[pl.BlockSpec((bd, LANES), col),
                  pl.BlockSpec((bd, gs * p), col), pl.BlockSpec((bd, gs * p), col),
                  pl.BlockSpec((1, gs * p), col), pl.BlockSpec((1, gs * p), col),
                  pl.BlockSpec((None, LANES, 2 * gs * p), slab),
                  pl.BlockSpec((None, gs * p, LANES), slab), pl.BlockSpec((None, gs * p, LANES), slab),
                  pl.BlockSpec((1, LANES), col)],
        out_specs=[pl.BlockSpec((bd, LANES), col),
                   pl.BlockSpec((bd, gs * p), col), pl.BlockSpec((bd, gs * p), col)],
        out_shape=[jax.ShapeDtypeStruct((bd, e), BF16), st, st],
        compiler_params=_params("parallel"),
        name="ssm_step",
    )(u, h_re, h_im, lb_re, lb_im, wb, crb, cib, d.reshape(1, g * ch))


def _diff_layer(xp, xs, k_pool, v_pool, j, page_table, g, w_in, lam_vec, subln_g, w_out, lam_init, final_g, seq,
                cache):
    wb = w_in.astype(BF16)
    wo = w_out.astype(BF16)
    lam = lam_vec.astype(F32)
    b = xp.shape[0] // seq
    e = xp.shape[1]
    past = page_table.shape[1] * PAGE_SIZE
    qbp, kp, kbp, vp, vtp, zp = _proj(xp, g, wb, "diff", _rope_tables(jnp.arange(seq)), True, cache)
    qs, ks, vs, zs = _proj(xs, g, wb, "diff", _rope_tables(jnp.full((xs.shape[0],), past)), False)
    r3 = lambda a: a.reshape(b, seq, e)
    op = _attn_prompt(r3(qbp), r3(kbp), vtp, "diff", (lam,), lam_init, seq).reshape(b * seq, e)
    os_ = _decode_attn(qs, ks, vs, k_pool, v_pool, page_table, "diff", (lam,), lam_init, j)
    xp = _oproj(op, zp, xp, wo, "diff", subln_g=subln_g, lam_init=lam_init, final_g=final_g)
    xs = _oproj(os_, zs, xs, wo, "diff", subln_g=subln_g, lam_init=lam_init, final_g=final_g)
    return xp, xs, kp, vp, ks, vs


def _fox_layer(xp, xs, k_pool, v_pool, lf_pool, j, page_table, g, w_in, b_f, w_out, seq, cache):
    nh = b_f.shape[0]
    wb = jnp.pad(w_in, ((0, 0), (0, LANES - nh))).astype(BF16)
    wo = w_out.astype(BF16)
    bf = b_f.reshape(1, nh).astype(F32)
    b = xp.shape[0] // seq
    e = xp.shape[1]
    qbp, kp, kbp, vp, vtp, zp, lfp = _proj(xp, g, wb, "fox", (bf,), True, cache)
    qs, ks, vs, zs, lfs = _proj(xs, g, wb, "fox", (bf,), False)
    r3 = lambda a: a.reshape(b, seq, e)
    bias = _fox_bias(lfp.reshape(b, seq, nh))
    op = _attn_prompt(r3(qbp), r3(kbp), vtp, "fox", (bias,), 0.0, seq).reshape(b * seq, e)
    os_ = _decode_attn(qs, ks, vs, k_pool, v_pool, page_table, "fox", (lfs.T, lf_pool), 0.0, j)
    xp = _oproj(op, zp, xp, wo, "fox")
    xs = _oproj(os_, zs, xs, wo, "fox")
    return xp, xs, kp, vp, lfp, ks, vs, lfs


def _ssm_layer(xp, xs, h0_re, h0_im, g, w_in, a_re, a_im, log_step, b_re, b_im, c_re, c_im, d, w_glu, w_out, seq):
    ng, p = a_re.shape
    ch, n = SSM_CH, SSM_CHUNK
    wb = w_in.astype(BF16)
    wg = w_glu.astype(BF16)
    wo = w_out.astype(BF16)
    b = xp.shape[0] // seq
    e = xp.shape[1]
    pw_re, pw_im, f_re, f_im = _ssm_discretize(a_re, a_im, log_step)
    t_tab, x_tab, f_tab, bb = _ssm_tables(pw_re, pw_im, f_re, f_im, b_re, b_im, c_re, c_im)
    dd = d.astype(F32)
    flat = lambda a: a[:, :p].reshape(1, ng * p)
    gs = LANES // ch
    ns = ng // gs
    t_slab = _slab_expand(t_tab, ns, gs, ch)
    e_slab = _slab_expand(x_tab, ns, gs, p)
    f_slab = jnp.swapaxes(_slab_expand(f_tab, ns, gs, p), 1, 2).reshape(ns, 2, gs * p, n * LANES)
    d_slab = jnp.broadcast_to(dd.reshape(ns, 1, gs * ch), (ns, n, gs * ch)).reshape(ns, 1, n * LANES)
    up, zp = _proj(xp, g, wb, "ssm", (), True)
    nchunk = b * seq // n
    cb = min(SSM_CHUNK_ROWS, nchunk)
    s_re, s_im = _ssm_local(up, e_slab, cb)
    h_re, h_im, fin_re, fin_im = _ssm_scan(s_re, s_im, flat(pw_re[n]), flat(pw_im[n]), b)
    gp = _ssm_out(up, t_slab, f_slab, d_slab, h_re, h_im, cb)
    xp = _oproj(gp, zp, xp, wo, "ssm", wg=wg)
    us, zs = _proj(xs, g, wb, "ssm", (), False)
    bd = xs.shape[0]
    gs, ns_re, ns_im = _ssm_step(us, h0_re.reshape(bd, ng * p).astype(F32), h0_im.reshape(bd, ng * p).astype(F32),
                                 flat(pw_re[1]), flat(pw_im[1]), bb, c_re.astype(F32), c_im.astype(F32), dd)
    xs = _oproj(gs, zs, xs, wo, "ssm", wg=wg)
    return xp, xs, fin_re.reshape(b, ng, p), fin_im.reshape(b, ng, p), ns_re.reshape(bd, ng, p), ns_im.reshape(bd, ng, p)


def kernel(x_prompt, x_sample, cache_diff_k, cache_diff_v, cache_fox_k, cache_fox_v, cache_fox_logf, state_ssm_re, state_ssm_im, page_table, norm_g, final_norm_g, diff_w_in, diff_lambda, diff_subln_g, diff_w_out, fox_w_in, fox_b_f, fox_w_out, ssm_w_in, ssm_a_re, ssm_a_im, ssm_log_step, ssm_b_re, ssm_b_im, ssm_c_re, ssm_c_im, ssm_d, ssm_w_glu, ssm_w_out):
    b, seq, dm = x_prompt.shape
    bd = x_sample.shape[0]
    depth = norm_g.shape[0]
    xp = x_prompt.reshape(b * seq, dm)
    xs = x_sample.reshape(bd, dm)
    rows = lambda c: c.reshape(c.shape[0], c.shape[1], c.shape[2] * c.shape[3], c.shape[4])
    cols = lambda c: jnp.transpose(c, (0, 1, 3, 4, 2)).reshape(c.shape[0], c.shape[1], c.shape[3] * c.shape[4],
                                                                c.shape[2])
    dk_pool, dv_pool = rows(cache_diff_k), rows(cache_diff_v)
    fk_pool, fv_pool = cols(cache_fox_k), cols(cache_fox_v)
    fl_pool = jnp.transpose(cache_fox_logf, (0, 1, 3, 2))
    acc = {name: [] for name in ("dks", "dvs", "flp", "fks", "fvs", "fls", "srp", "sip", "srs", "sis")}
    ha, hb = cache_diff_k.shape[3], cache_fox_k.shape[3]
    diff_kv = fox_kv = None
    for i in range(depth):
        kind, j = i % N_MIXERS, i // N_MIXERS
        final_g = final_norm_g if i == depth - 1 else None
        if kind == 0:
            xp, xs, kp, vp, ks, vs = _diff_layer(
                xp, xs, dk_pool, dv_pool, j, page_table, norm_g[i], diff_w_in[j], diff_lambda[j],
                diff_subln_g[j], diff_w_out[j], _diff_lambda_init(i), final_g, seq,
                (j, diff_w_in.shape[0], diff_kv, b))
            diff_kv = (kp, vp)
            acc["dks"].append(ks); acc["dvs"].append(vs)
        elif kind == 1:
            xp, xs, kp, vp, lfp, ks, vs, lfs = _fox_layer(
                xp, xs, fk_pool, fv_pool, fl_pool, j, page_table, norm_g[i], fox_w_in[j], fox_b_f[j],
                fox_w_out[j], seq, (j, fox_w_in.shape[0], fox_kv, b))
            fox_kv = (kp, vp)
            acc["flp"].append(lfp)
            acc["fks"].append(ks); acc["fvs"].append(vs); acc["fls"].append(lfs)
        else:
            xp, xs, hrp, hip, hrs, his = _ssm_layer(
                xp, xs, state_ssm_re[j], state_ssm_im[j], norm_g[i], ssm_w_in[j], ssm_a_re[j], ssm_a_im[j],
                ssm_log_step[j], ssm_b_re[j], ssm_b_im[j], ssm_c_re[j], ssm_c_im[j], ssm_d[j], ssm_w_glu[j],
                ssm_w_out[j], seq)
            acc["srp"].append(hrp); acc["sip"].append(hip); acc["srs"].append(hrs); acc["sis"].append(his)
    stack = lambda name, shape: jnp.stack(acc[name]).reshape((len(acc[name]),) + shape)
    diff_rows = lambda a: a.reshape(a.shape[0], b, seq, ha, dm // ha)
    fox_rows = lambda a: a.reshape(a.shape[0], b, hb, dm // hb, seq).transpose(0, 1, 4, 2, 3)
    return (xp.reshape(b, seq, dm), xs.reshape(bd, 1, dm),
            diff_rows(diff_kv[0]), diff_rows(diff_kv[1]),
            stack("dks", (bd, 1, ha, dm // ha)), stack("dvs", (bd, 1, ha, dm // ha)),
            fox_rows(fox_kv[0]), fox_rows(fox_kv[1]),
            stack("flp", (b, seq, hb)),
            stack("fks", (bd, 1, hb, dm // hb)), stack("fvs", (bd, 1, hb, dm // hb)),
            stack("fls", (bd, 1, hb)),
            jnp.stack(acc["srp"]), jnp.stack(acc["sip"]), jnp.stack(acc["srs"]), jnp.stack(acc["sis"]))
```

```python
import functools
import math

import numpy as np

import jax
import jax.numpy as jnp
from jax import lax
from jax.experimental import pallas as pl
from jax.experimental.pallas import tpu as pltpu

F32 = jnp.float32
BF16 = jnp.bfloat16
HIGHEST = lax.Precision.HIGHEST

LANES = 128
SUBLANES = 8
D_MODEL = 1024
PAGE_SIZE = 128
N_MIXERS = 3
D_HEAD = 64
SSM_CH = 16
SSM_P = 64
SSM_CHUNK = 16
NORM_EPS = 1e-6
SUBLN_EPS = 1e-5
NEG_INF = -1e30
ROPE_THETA = 10000.0
LOG2E = math.log2(math.e)
QK_SCALE = D_HEAD ** -0.5 * LOG2E
BIAS_PIECES = 3
BF16_ROWS = 16
VT_ROWS = LANES + BF16_ROWS
VMEM_LIMIT = 56 * 1024 * 1024

PROJ_ROWS = 256
OPROJ_ROWS = 512
CUMSUM_ROWS = 256
SSM_CHUNK_ROWS = 256
ATTN_TILE = 2048
ATTN_QSUB = 256
ATTN_AHEAD = 6

NT_DIMS = (((1,), (1,)), ((), ()))


def _params(*sem):
    return pltpu.CompilerParams(dimension_semantics=sem, vmem_limit_bytes=VMEM_LIMIT)


def _diff_lambda_init(layer_idx):
    return 0.8 - 0.6 * math.exp(-0.3 * layer_idx)


def _rms(x, g, eps):
    return x * lax.rsqrt(jnp.mean(x * x, axis=-1, keepdims=True) + eps) * g


def _silu(z):
    return z * jax.nn.sigmoid(z)


def _diff_lambda(lv, lam_init):
    return (jnp.exp(jnp.sum(lv[0:1] * lv[1:2], axis=1, keepdims=True))
            - jnp.exp(jnp.sum(lv[2:3] * lv[3:4], axis=1, keepdims=True)) + lam_init)


def _rope(a, cos, sin_signed, first_half):
    swapped = jnp.where(first_half, pltpu.roll(a, 96, 1), pltpu.roll(a, 32, 1))
    return a * cos + swapped * sin_signed


def _gated(a_ref, z, extra_ref, mode, lam_init):
    e = D_MODEL
    if mode == "diff":
        a = a_ref[...]
        sg = extra_ref[...] * (1.0 - lam_init)
        a = jnp.concatenate(
            [_rms(a[:, h * LANES:(h + 1) * LANES], 1.0, SUBLN_EPS) * sg for h in range(e // LANES)], axis=1)
    elif mode == "ssm":
        gl = jnp.dot(a_ref[...].astype(BF16), extra_ref[...], preferred_element_type=F32)
        a = gl[:, :e] * jax.nn.sigmoid(gl[:, e:])
    else:
        a = a_ref[...]
    return (a * _silu(z)).astype(BF16)


def _proj_body(*refs, mode, prompt, fused=None):
    e = D_MODEL
    refs = list(refs)
    x_ref, g_ref, w_ref = refs[:3]
    refs = refs[3:]
    x = x_ref[...]
    if fused is not None:
        prev_mode, prev_lam_init = fused
        a_ref, zp_ref, wo_ref = refs[:3]
        refs = refs[3:]
        extra_ref = refs.pop(0) if prev_mode in ("diff", "ssm") else None
        xo_ref = refs.pop()
        gated = _gated(a_ref, zp_ref[...], extra_ref, prev_mode, prev_lam_init)
        x = x + jnp.dot(gated, wo_ref[...], preferred_element_type=F32)
        xo_ref[...] = x
    xn = _rms(x, g_ref[...], NORM_EPS).astype(BF16)

    def proj(lo, width=e):
        return jnp.dot(xn, w_ref[:, lo:lo + width], preferred_element_type=F32)

    if mode == "ssm":
        u_ref, z_ref = refs
        u_ref[...] = proj(0)
        z_ref[...] = proj(e)
        return
    if mode == "diff":
        cos_ref, sin_ref = refs[:2]
        refs = refs[2:]
    else:
        bf_ref = refs.pop(0)
        lf_ref = refs.pop()
    if prompt:
        q_ref, k_ref, kb_ref, v_ref, vt_ref, z_ref = refs[-6:]
    else:
        q_ref, k_ref, v_ref, z_ref = refs
    q = proj(0) * QK_SCALE
    k = proj(e)
    if mode == "diff":
        cos = cos_ref[...]
        sin = sin_ref[...]
        lane = lax.broadcasted_iota(jnp.int32, (1, LANES), 1)
        first_half = (lane % D_HEAD) < (D_HEAD // 2)
        blocks = [slice(h * LANES, (h + 1) * LANES) for h in range(e // LANES)]
        q = jnp.concatenate([_rope(q[:, sl], cos, sin, first_half) for sl in blocks], axis=1)
        k = jnp.concatenate([_rope(k[:, sl], cos, sin, first_half) for sl in blocks], axis=1)
    q_ref[...] = q.astype(q_ref.dtype)
    v = proj(2 * e)
    if not prompt:
        k_ref[...] = k
        v_ref[...] = v
    else:
        kb_ref[...] = k.astype(BF16)
        ones = jnp.ones((BF16_ROWS, v.shape[0]), BF16)
        for h in range(e // LANES):
            sl = slice(h * LANES, (h + 1) * LANES)
            v_t = v[:, sl].T
            vt_ref[h * VT_ROWS:h * VT_ROWS + LANES, :] = v_t.astype(BF16)
            vt_ref[h * VT_ROWS + LANES:(h + 1) * VT_ROWS, :] = ones
            if mode == "fox":
                k_ref[sl, :] = k[:, sl].T
                v_ref[sl, :] = v_t
            else:
                k_ref[:, h, :] = k[:, sl]
                v_ref[:, h, :] = v[:, sl]
    z_ref[...] = proj(3 * e)
    if mode == "fox":
        n_heads = lf_ref.shape[-1]
        f = proj(4 * e, LANES)[:, :n_heads] + bf_ref[...]
        lf_ref[...] = jax.nn.log_sigmoid(f)


def _proj(x, g, w, mode, extras, prompt, cache=None, fuse=None):
    t, d = x.shape
    n = w.shape[1]
    tm = min(t, PROJ_ROWS)
    row = lambda i: (i, 0)
    fixed = lambda i: (0, 0)
    args = [x, g.reshape(1, d), w]
    in_specs = [pl.BlockSpec((tm, d), row), pl.BlockSpec((1, d), fixed), pl.BlockSpec((d, n), fixed)]
    if fuse is not None:
        a, zp, wo, _, extra, _ = fuse
        args += [a, zp, wo]
        in_specs += [pl.BlockSpec((tm, d), row), pl.BlockSpec((tm, d), row), pl.BlockSpec(wo.shape, fixed)]
        if extra is not None:
            args.append(extra)
            in_specs.append(pl.BlockSpec(extra.shape, fixed))
    args += list(extras)
    f32_out = (jax.ShapeDtypeStruct((t, d), F32), pl.BlockSpec((tm, d), row))
    bf_out = (jax.ShapeDtypeStruct((t, d), BF16), pl.BlockSpec((tm, d), row))
    aliases = {}
    if mode == "ssm":
        outs = [f32_out, f32_out]
    else:
        if mode == "diff":
            nblk = extras[0].shape[0] // tm
            tab = pl.BlockSpec((tm, LANES), lambda i: (i % nblk, 0))
            in_specs += [tab, tab]
        else:
            in_specs += [pl.BlockSpec(extras[0].shape, fixed)]
        if prompt:
            slot, n_slots, prev, batch = cache
            vt_rows = d // LANES * VT_ROWS
            vt_out = (jax.ShapeDtypeStruct((vt_rows, t), BF16), pl.BlockSpec((vt_rows, tm), lambda i: (0, i)))
            if mode == "fox":
                nb = t // batch // tm
                kv_out = (jax.ShapeDtypeStruct((n_slots, batch, d, t // batch), F32),
                          pl.BlockSpec((None, None, d, tm), lambda i: (slot, i // nb, 0, i % nb)))
            else:
                kv_out = (jax.ShapeDtypeStruct((n_slots, t, d // LANES, LANES), F32),
                          pl.BlockSpec((None, tm, d // LANES, LANES), lambda i: (slot, i, 0, 0)))
            outs = [bf_out, kv_out, bf_out, kv_out, vt_out, f32_out]
            if prev is not None:
                aliases = {len(args): 1, len(args) + 1: 3}
                args += list(prev)
                in_specs += [pl.BlockSpec(memory_space=pl.ANY)] * 2
        else:
            outs = [f32_out] * 4
        if mode == "fox":
            nh = extras[0].shape[1]
            outs.append((jax.ShapeDtypeStruct((t, nh), F32), pl.BlockSpec((tm, nh), row)))
    fused = None
    if fuse is not None:
        outs.append(f32_out)
        fused = (fuse[3], fuse[5])
    return pl.pallas_call(
        functools.partial(_proj_body, mode=mode, prompt=prompt, fused=fused),
        grid=(t // tm,),
        in_specs=in_specs,
        out_specs=[o[1] for o in outs],
        out_shape=[o[0] for o in outs],
        input_output_aliases=aliases,
        compiler_params=_params("parallel"),
        name=f"proj_{mode}",
    )(*args)


def _rope_tables(pos):
    half = D_HEAD // 2
    inv = ROPE_THETA ** (-jnp.arange(half, dtype=F32) / half)
    ang = pos.astype(F32)[:, None] * inv[None, :]
    cos, sin = jnp.cos(ang), jnp.sin(ang)
    cos = jnp.concatenate([cos, cos, cos, cos], axis=1)
    sin = jnp.concatenate([-sin, sin, -sin, sin], axis=1)
    return cos, sin


def _colmax(st, group=4 * SUBLANES):
    r, c = st.shape
    return jnp.max(jnp.max(st.reshape(r // group, group, c), axis=0), axis=0, keepdims=True)


def _attn_body(qi_ref, kj_ref, *refs, kind, lam_init, tq, qsub):
    if kind == "diff":
        q_ref, k_ref, vt_ref, lam_ref, o_ref, m_sc, acc_sc = refs
    else:
        q_ref, qx_ref, k_ref, kx_ref, vt_ref, o_ref, m_sc, acc_sc, qx_sc = refs
    n = pl.program_id(2)
    qi = qi_ref[n]
    kj = kj_ref[n]
    lane = lax.broadcasted_iota(jnp.int32, (1, LANES), 1)
    low = lane < D_HEAD

    @pl.when(kj == 0)
    def _init():
        m_sc[...] = jnp.full(m_sc.shape, NEG_INF, F32)
        acc_sc[...] = jnp.zeros(acc_sc.shape, F32)
        if kind == "fox":
            w = BIAS_PIECES
            for s in range(2):
                cq = pltpu.roll(qx_ref[...].astype(F32), (2 - s) * w, 1)
                own = (lane >= s * w) & (lane < (s + 1) * w)
                ones = (lane >= 2 * w) & (lane < 3 * w)
                qx_sc[s] = jnp.where(ones, cq, jnp.where(own, -1.0, 0.0)).astype(BF16)

    def step(diagonal):
        q = q_ref[...]
        k = k_ref[...]
        if kind == "fox":
            k = jnp.concatenate([k, kx_ref[...]], axis=1)
        vt = vt_ref[...]
        qs = []
        for s in range(2):
            qm = jnp.where(low if s == 0 else jnp.logical_not(low), q, jnp.zeros_like(q))
            if kind == "fox":
                qm = jnp.concatenate([qm, qx_sc[s]], axis=1)
            qs.append(qm)
        chains = [(s, j) for s in range(2) for j in range(tq // qsub)]

        def scores(s, j):
            cs = slice(j * qsub, (j + 1) * qsub)
            rows = (j + 1) * qsub if diagonal else tq
            st = lax.dot_general(k[:rows], qs[s][cs], NT_DIMS, preferred_element_type=F32)
            if diagonal:
                kpos = lax.broadcasted_iota(jnp.int32, st.shape, 0)
                qpos = j * qsub + lax.broadcasted_iota(jnp.int32, st.shape, 1)
                st = jnp.where(kpos <= qpos, st, NEG_INF)
            return s, cs, rows, st

        def softmax_values(s, cs, rows, st):
            m_prev = m_sc[s, :, cs]
            m_new = jnp.maximum(m_prev, _colmax(st))
            alpha = jnp.exp2(m_prev - m_new)
            p = jnp.exp2(st - m_new).astype(BF16)
            acc_sc[s, :, cs] = alpha * acc_sc[s, :, cs] + jnp.dot(vt[:, :rows], p, preferred_element_type=F32)
            m_sc[s, :, cs] = m_new

        pending = [scores(*c) for c in chains[:ATTN_AHEAD]]
        for c in chains[ATTN_AHEAD:]:
            softmax_values(*pending.pop(0))
            pending.append(scores(*c))
        for done in pending:
            softmax_values(*done)

    @pl.when(kj < qi)
    def _full():
        step(False)

    @pl.when(kj == qi)
    def _diag():
        step(True)
        o = [acc_sc[s, :LANES, :] / acc_sc[s, LANES:LANES + 1, :] for s in range(2)]
        if kind == "diff":
            w = o[0] - _diff_lambda(lam_ref[...], lam_init) * o[1]
        else:
            row = lax.broadcasted_iota(jnp.int32, o[0].shape, 0)
            w = jnp.where(row < D_HEAD, o[0], o[1])
        o_ref[...] = w.T


def _attn_prompt(qb, kb, vt, kind, extras, lam_init, seq):
    b, l, e = qb.shape
    tq, qsub = min(ATTN_TILE, seq), min(ATTN_QSUB, seq)
    nblk = l // tq
    steps = [(i, j) for i in range(nblk) for j in range(i + 1)]
    qi, kj = (jnp.asarray([s[c] for s in steps], jnp.int32) for c in range(2))
    q_spec = pl.BlockSpec((None, tq, LANES), lambda bb, h, n, qi, kj: (bb, qi[n], h))
    k_spec = pl.BlockSpec((None, tq, LANES), lambda bb, h, n, qi, kj: (bb, kj[n], h))
    vt_spec = pl.BlockSpec((VT_ROWS, tq), lambda bb, h, n, qi, kj: (h, bb * nblk + kj[n]))
    scratch = [pltpu.VMEM((2, 1, tq), F32), pltpu.VMEM((2, VT_ROWS, tq), F32)]
    if kind == "diff":
        (lam,) = extras
        args = [qb, kb, vt, lam]
        in_specs = [q_spec, k_spec, vt_spec, pl.BlockSpec(lam.shape, lambda bb, h, n, qi, kj: (0, 0))]
    else:
        (cx,) = extras
        args = [qb, cx, kb, cx, vt]
        in_specs = [q_spec, q_spec, k_spec, k_spec, vt_spec]
        scratch.append(pltpu.VMEM((2, tq, LANES), BF16))
    return pl.pallas_call(
        functools.partial(_attn_body, kind=kind, lam_init=lam_init, tq=tq, qsub=qsub),
        grid_spec=pltpu.PrefetchScalarGridSpec(
            num_scalar_prefetch=2,
            grid=(b, e // LANES, len(steps)),
            in_specs=in_specs,
            out_specs=q_spec,
            scratch_shapes=scratch),
        out_shape=jax.ShapeDtypeStruct((b, l, e), F32),
        compiler_params=_params("parallel", "parallel", "arbitrary"),
        name=f"attn_{kind}",
    )(qi, kj, *args)


def _fox_bias_body(lf_ref, sel_ref, const_ref, cx_ref, carry_sc):
    @pl.when(pl.program_id(1) == 0)
    def _():
        carry_sc[...] = jnp.zeros(carry_sc.shape, F32)

    lf = lf_ref[...]
    t = lf.shape[0]
    r = lax.broadcasted_iota(jnp.int32, (t, t), 0)
    c = lax.broadcasted_iota(jnp.int32, (t, t), 1)
    tri = (c <= r).astype(F32)
    cs = jnp.dot(tri, lf, preferred_element_type=F32, precision=HIGHEST) + carry_sc[...]
    carry_sc[...] = cs[t - 1:t, :]
    c2 = cs * LOG2E
    hi = c2.astype(BF16)
    rest = c2 - hi.astype(F32)
    mid = rest.astype(BF16)
    lo = (rest - mid.astype(F32)).astype(BF16)
    placed = const_ref[...]
    for piece, val in enumerate((hi, mid, lo)):
        placed = placed + jnp.dot(val, sel_ref[piece], preferred_element_type=F32)
    cx_ref[...] = placed.astype(BF16)


def _fox_bias_layout(nh):
    sel = np.zeros((3, nh, nh // 2 * LANES), np.float32)
    const = np.zeros((1, nh // 2 * LANES), np.float32)
    for head in range(nh):
        base, s = head // 2 * LANES, head % 2
        for piece in range(3):
            sel[piece, head, base + BIAS_PIECES * s + piece] = 1.0
            const[0, base + 2 * BIAS_PIECES + piece] = 1.0
    return jnp.asarray(sel, BF16), jnp.asarray(const, F32)


def _fox_bias(logf):
    b, l, nh = logf.shape
    tm = min(l, CUMSUM_ROWS)
    sel, const = _fox_bias_layout(nh)
    width = sel.shape[-1]
    return pl.pallas_call(
        _fox_bias_body,
        grid=(b, l // tm),
        in_specs=[pl.BlockSpec((None, tm, nh), lambda bb, i: (bb, i, 0)),
                  pl.BlockSpec(sel.shape, lambda bb, i: (0, 0, 0)),
                  pl.BlockSpec(const.shape, lambda bb, i: (0, 0))],
        out_specs=pl.BlockSpec((None, tm, width), lambda bb, i: (bb, i, 0)),
        out_shape=jax.ShapeDtypeStruct((b, l, width), BF16),
        scratch_shapes=[pltpu.VMEM((1, nh), F32)],
        compiler_params=_params("parallel", "arbitrary"),
        name="fox_bias",
    )(logf, sel, const)


def _oproj_body(*refs, mode, lam_init, final):
    refs = list(refs)
    a_ref, z_ref, x_ref = refs[:3]
    refs = refs[3:]
    if mode == "ssm":
        wg_ref = refs.pop(0)
    w_ref = refs.pop(0)
    if mode == "diff":
        sg_ref = refs.pop(0)
    if final:
        fg_ref = refs.pop(0)
    out_ref = refs.pop(0)
    e = D_MODEL
    z = z_ref[...]
    if mode == "diff":
        a = a_ref[...]
        sg = sg_ref[...] * (1.0 - lam_init)
        a = jnp.concatenate(
            [_rms(a[:, h * LANES:(h + 1) * LANES], 1.0, SUBLN_EPS) * sg for h in range(e // LANES)], axis=1)
    elif mode == "ssm":
        gl = jnp.dot(a_ref[...].astype(BF16), wg_ref[...], preferred_element_type=F32)
        a = gl[:, :e] * jax.nn.sigmoid(gl[:, e:])
    else:
        a = a_ref[...]
    y = jnp.dot((a * _silu(z)).astype(BF16), w_ref[...], preferred_element_type=F32)
    xn = x_ref[...] + y
    if final:
        out_ref[...] = _rms(xn, fg_ref[...], NORM_EPS)
    else:
        out_ref[...] = xn


def _oproj(a, z, x, w, mode, wg=None, subln_g=None, lam_init=0.0, final_g=None):
    t, d = x.shape
    tm = min(t, OPROJ_ROWS)
    row = lambda i: (i, 0)
    fixed = lambda i: (0, 0)
    blk = pl.BlockSpec((tm, d), row)
    args = [a, z, x]
    in_specs = [blk, blk, blk]
    if mode == "ssm":
        args.append(wg)
        in_specs.append(pl.BlockSpec(wg.shape, fixed))
    args.append(w)
    in_specs.append(pl.BlockSpec(w.shape, fixed))
    if mode == "diff":
        args.append(subln_g.reshape(1, LANES))
        in_specs.append(pl.BlockSpec((1, LANES), fixed))
    if final_g is not None:
        args.append(final_g.reshape(1, d))
        in_specs.append(pl.BlockSpec((1, d), fixed))
    return pl.pallas_call(
        functools.partial(_oproj_body, mode=mode, lam_init=lam_init, final=final_g is not None),
        grid=(t // tm,),
        in_specs=in_specs,
        out_specs=blk,
        out_shape=jax.ShapeDtypeStruct((t, d), F32),
        compiler_params=_params("parallel"),
        name=f"oproj_{mode}",
    )(*args)


def _decode_diff_body(pt_ref, q_ref, kn_ref, vn_ref, lam_ref, *refs, lam_init, n_pages):
    del pt_ref
    k_refs = refs[:n_pages]
    v_refs = refs[n_pages:2 * n_pages]
    o_ref = refs[2 * n_pages]
    q8 = q_ref[...]
    nh = q8.shape[0]
    low = lax.broadcasted_iota(jnp.int32, q8.shape, 1) < D_HEAD
    q = jnp.concatenate([jnp.where(low, q8, 0.0), jnp.where(low, 0.0, q8)], axis=0)
    qb = q.astype(BF16)
    rows = k_refs[0].shape[0]
    own = (lax.broadcasted_iota(jnp.int32, (2 * nh, rows), 1) % nh
           == lax.broadcasted_iota(jnp.int32, (2 * nh, rows), 0) % nh)
    s = jnp.concatenate(
        [jnp.where(own, lax.dot_general(qb, k_refs[p][...].astype(BF16), NT_DIMS, preferred_element_type=F32),
                   NEG_INF) for p in range(n_pages)], axis=1)
    kn = kn_ref[...]
    vn = vn_ref[...]
    s_new = jnp.sum(q * jnp.concatenate([kn, kn], axis=0), axis=1, keepdims=True)
    m = jnp.maximum(jnp.max(s, axis=1, keepdims=True), s_new)
    p_past = jnp.exp2(s - m)
    p_new = jnp.exp2(s_new - m)
    l = jnp.sum(p_past, axis=1, keepdims=True) + p_new
    pb = p_past.astype(BF16)
    acc = p_new * jnp.concatenate([vn, vn], axis=0)
    for p in range(n_pages):
        acc = acc + jnp.dot(pb[:, p * rows:(p + 1) * rows], v_refs[p][...].astype(BF16),
                            preferred_element_type=F32)
    o = acc / l
    o_ref[...] = o[:nh] - _diff_lambda(lam_ref[...], lam_init) * o[nh:]


def _decode_fox_body(pt_ref, q_ref, kn_ref, vn_ref, lfn_ref, *refs, n_pages):
    del pt_ref
    kt_refs = refs[:n_pages]
    vt_refs = refs[n_pages:2 * n_pages]
    lf_refs = refs[2 * n_pages:3 * n_pages]
    o_ref = refs[3 * n_pages]
    e = q_ref.shape[-1]
    nh = e // D_HEAD
    b = pl.program_id(0)
    row = lax.broadcasted_iota(jnp.int32, (nh, e), 0)
    lane = lax.broadcasted_iota(jnp.int32, (nh, e), 1)
    own = (lane // D_HEAD) == row
    q = jnp.where(own, q_ref[...], 0.0)
    qb = q.astype(BF16)
    s = jnp.concatenate(
        [jnp.dot(qb, kt_refs[p][...].astype(BF16), preferred_element_type=F32) for p in range(n_pages)], axis=1)
    s_new = jnp.sum(q * kn_ref[...], axis=1, keepdims=True)
    page = lf_refs[0].shape[1]
    lft = jnp.concatenate([lf_refs[p][...] for p in range(n_pages)], axis=0)
    r = lax.broadcasted_iota(jnp.int32, (page, page), 0)
    c = lax.broadcasted_iota(jnp.int32, (page, page), 1)
    within = jnp.dot(lft, (r <= c).astype(F32), preferred_element_type=F32, precision=HIGHEST)
    offs = jnp.zeros((nh, 1), F32)
    c_past = []
    for p in range(n_pages):
        w = within[p * nh:(p + 1) * nh]
        c_past.append(w + offs)
        offs = offs + w[:, page - 1:page]
    c_past = jnp.concatenate(c_past, axis=1)
    lfn = lfn_ref[...]
    bcol = lax.broadcasted_iota(jnp.int32, lfn.shape, 1)
    c_new = offs + jnp.sum(jnp.where(bcol == b, lfn, 0.0), axis=1, keepdims=True)
    s = s + (c_new - c_past) * LOG2E
    m = jnp.maximum(jnp.max(s, axis=1, keepdims=True), s_new)
    p_past = jnp.exp2(s - m)
    p_new = jnp.exp2(s_new - m)
    l = jnp.sum(p_past, axis=1, keepdims=True) + p_new
    pb = p_past.astype(BF16)
    acc = p_new * vn_ref[...]
    for p in range(n_pages):
        acc = acc + lax.dot_general(pb[:, p * page:(p + 1) * page], vt_refs[p][...].astype(BF16), NT_DIMS,
                                    preferred_element_type=F32)
    o_ref[...] = jnp.sum(jnp.where(own, acc / l, 0.0), axis=0, keepdims=True)


def _decode_attn(q, k_new, v_new, k_pool, v_pool, page_table, kind, extras, lam_init, layer):
    bd, e = q.shape
    n_pages = page_table.shape[1]
    if kind == "diff":
        vec_shape = (bd, e // LANES, LANES)
    else:
        vec_shape = (bd, 1, e)
    vec = pl.BlockSpec((None,) + vec_shape[1:], lambda b, pt: (b, 0, 0))
    in_specs = [vec, vec, vec]
    args = [q.reshape(vec_shape), k_new.reshape(vec_shape), v_new.reshape(vec_shape)]
    pools = [k_pool, v_pool]
    if kind == "diff":
        (lam,) = extras
        args.append(lam)
        in_specs.append(pl.BlockSpec(lam.shape, lambda b, pt: (0, 0)))
        body = functools.partial(_decode_diff_body, lam_init=lam_init, n_pages=n_pages)
    else:
        lf_new_t, lf_pool = extras
        args.append(lf_new_t)
        in_specs.append(pl.BlockSpec(lf_new_t.shape, lambda b, pt: (0, 0)))
        pools.append(lf_pool)
        body = functools.partial(_decode_fox_body, n_pages=n_pages)

    def page_spec(pool, p):
        return pl.BlockSpec((None, None) + pool.shape[2:], lambda b, pt: (layer, pt[b, p], 0, 0))

    for pool in pools:
        args += [pool] * n_pages
        in_specs += [page_spec(pool, p) for p in range(n_pages)]
    out = pl.pallas_call(
        body,
        grid_spec=pltpu.PrefetchScalarGridSpec(
            num_scalar_prefetch=1,
            grid=(bd,),
            in_specs=in_specs,
            out_specs=vec),
        out_shape=jax.ShapeDtypeStruct(vec_shape, F32),
        compiler_params=_params("parallel"),
        name=f"decode_{kind}",
    )(page_table, *args)
    return out.reshape(bd, e)


def _ssm_disc_body(are_ref, aim_ref, dt_ref, pr_ref, pi_ref, fre_ref, fim_ref):
    a_re = jnp.minimum(are_ref[...], -1e-4)
    a_im = aim_ref[...]
    dt = dt_ref[...]
    mag = jnp.exp(a_re * dt)
    ang = a_im * dt
    lb_re = mag * jnp.cos(ang)
    lb_im = mag * jnp.sin(ang)
    den = a_re * a_re + a_im * a_im
    num_re = lb_re - 1.0
    num_im = lb_im
    fre_ref[...] = (num_re * a_re + num_im * a_im) / den
    fim_ref[...] = (num_im * a_re - num_re * a_im) / den
    pr = jnp.ones_like(lb_re)
    pi = jnp.zeros_like(lb_im)
    for j in range(SSM_CHUNK + 1):
        pr_ref[j] = pr
        pi_ref[j] = pi
        pr, pi = pr * lb_re - pi * lb_im, pr * lb_im + pi * lb_re


def _twice(a):
    return jnp.concatenate([a, a], axis=-1).astype(F32)


def _ssm_discretize(a_re, a_im, log_step):
    g, p = a_re.shape
    dt = jnp.broadcast_to(jnp.exp(log_step.astype(F32))[:, None], (g, 2 * p))
    pw = jax.ShapeDtypeStruct((SSM_CHUNK + 1, g, 2 * p), F32)
    gp = jax.ShapeDtypeStruct((g, 2 * p), F32)
    return pl.pallas_call(_ssm_disc_body, out_shape=[pw, pw, gp, gp], name="ssm_discretize")(
        _twice(a_re), _twice(a_im), dt)


def _ssm_tables_body(pr_ref, pi_ref, fre_ref, fim_ref, br_ref, bi_ref, cr_ref, ci_ref,
                     t_ref, x_ref, f_ref, bb_ref):
    n = SSM_CHUNK
    ch = SSM_CH
    low = lax.broadcasted_iota(jnp.int32, (1, 2 * SSM_P), 1) < SSM_P
    fre = fre_ref[...]
    fim = fim_ref[...]
    br = br_ref[...]
    bi = bi_ref[...]
    bbr = fre * br - fim * bi
    bbi = fre * bi + fim * br
    bb_ref[...] = jnp.where(low, bbr, bbi)
    cr = cr_ref[...]
    ci = ci_ref[...]
    x_rows = []
    for j in range(n):
        pr = pr_ref[n - 1 - j]
        pi = pi_ref[n - 1 - j]
        x_rows.append(bbr * jnp.where(low, pr, pi) + bbi * jnp.where(low, -pi, pr))
    x_all = jnp.concatenate(x_rows, axis=0)
    x_ref[...] = x_all
    c_cat = jnp.where(low, cr, -ci)
    c_rep = jnp.concatenate([c_cat] * n, axis=0)
    k_rev = lax.dot_general(x_all, c_rep, NT_DIMS, preferred_element_type=F32, precision=HIGHEST)
    blk = lax.broadcasted_iota(jnp.int32, (n * ch, n * ch), 1) // ch
    toe = jnp.zeros((n * ch, n * ch), F32)
    for t in range(n):
        sh = ch * (n - 1 - t)
        shifted = k_rev if sh == 0 else jnp.concatenate([k_rev[sh:], jnp.zeros((sh, n * ch), F32)], axis=0)
        toe = jnp.where(blk == t, shifted, toe)
    t_ref[...] = toe
    f_rows = []
    for t in range(n):
        pr = pr_ref[t + 1]
        pi = pi_ref[t + 1]
        f_rows.append(cr * jnp.where(low, pr, -pi) + ci * jnp.where(low, -pi, -pr))
    f_ref[...] = jnp.concatenate(f_rows, axis=0)


def _ssm_tables(pw_re, pw_im, f_re, f_im, b_re, b_im, c_re, c_im):
    g, p2 = f_re.shape
    n, ch = SSM_CHUNK, SSM_CH
    nc = n * ch
    gmap = lambda i: (i, 0, 0)
    pw_spec = pl.BlockSpec((n + 1, None, 1, p2), lambda i: (0, i, 0, 0))
    row_spec = pl.BlockSpec((None, 1, p2), gmap)
    mat_spec = pl.BlockSpec((None, ch, p2), gmap)
    out = lambda rows, cols: (jax.ShapeDtypeStruct((g, rows, cols), F32), pl.BlockSpec((None, rows, cols), gmap))
    outs = [out(nc, nc), out(nc, p2), out(nc, p2), out(ch, p2)]
    return pl.pallas_call(
        _ssm_tables_body,
        grid=(g,),
        in_specs=[pw_spec, pw_spec, row_spec, row_spec, mat_spec, mat_spec, mat_spec, mat_spec],
        out_specs=[o[1] for o in outs],
        out_shape=[o[0] for o in outs],
        compiler_params=_params("parallel"),
        name="ssm_tables",
    )(pw_re.reshape(n + 1, g, 1, p2), pw_im.reshape(n + 1, g, 1, p2), f_re.reshape(g, 1, p2), f_im.reshape(g, 1, p2),
      _twice(jnp.swapaxes(b_re, 1, 2)), _twice(jnp.swapaxes(b_im, 1, 2)), _twice(c_re), _twice(c_im))


def _slab_expand(tab, ns, gs, minor):
    g, rows, cols = tab.shape
    n, ch = SSM_CHUNK, SSM_CH
    w = tab.reshape(ns, gs, n, ch, cols).transpose(0, 2, 1, 3, 4).reshape(ns, n * gs * ch, cols)
    x = cols // minor
    spread = np.zeros((x, minor, x, gs, minor), np.float32)
    for i in range(x):
        for m in range(minor):
            spread[i, m, i, :, m] = 1.0
    spread = jnp.asarray(spread.reshape(cols, x * gs * minor), BF16)
    row_g = (np.arange(n * gs * ch) // ch) % gs
    col_g = (np.arange(x * gs * minor) // minor) % gs
    keep = jnp.asarray(row_g[:, None] == col_g[None, :])
    z = jnp.einsum("arc,cd->ard", w.astype(BF16), spread, preferred_element_type=BF16)
    return jnp.where(keep[None], z, jnp.zeros_like(z))


def _chunk_rows(u_ref, cb):
    return jnp.concatenate([u_ref[pl.ds(t, cb, stride=SSM_CHUNK), :] for t in range(SSM_CHUNK)], axis=1)


def _ssm_local_body(u_ref, e_ref, sre_ref, sim_ref):
    cb = sre_ref.shape[0]
    s = jnp.dot(_chunk_rows(u_ref, cb).astype(BF16), e_ref[...], preferred_element_type=F32)
    half = s.shape[1] // 2
    sre_ref[...] = s[:, :half]
    sim_ref[...] = s[:, half:]


def _ssm_local(u, e_slab, cb):
    t, e = u.shape
    ns, kdim, width = e_slab.shape
    nchunk = t // SSM_CHUNK
    half = width // 2
    st = jax.ShapeDtypeStruct((nchunk, ns * half), F32)
    return pl.pallas_call(
        _ssm_local_body,
        grid=(ns, nchunk // cb),
        in_specs=[pl.BlockSpec((cb * SSM_CHUNK, LANES), lambda a, i: (i, a)),
                  pl.BlockSpec((None, kdim, width), lambda a, i: (a, 0, 0))],
        out_specs=[pl.BlockSpec((cb, half), lambda a, i: (i, a))] * 2,
        out_shape=[st, st],
        compiler_params=_params("parallel", "parallel"),
        name="ssm_local",
    )(u, e_slab)


def _ssm_scan_body(sre_ref, sim_ref, ar_ref, ai_ref, hre_ref, him_ref, fre_ref, fim_ref, *, per_seq):
    n_seq = sre_ref.shape[0] // per_seq
    ar = ar_ref[...]
    ai = ai_ref[...]
    zero = jnp.zeros_like(ar)

    def body(i, carry):
        new = []
        for q in range(n_seq):
            hr, hi = carry[2 * q], carry[2 * q + 1]
            row = q * per_seq + i
            hre_ref[pl.ds(row, 1), :] = hr
            him_ref[pl.ds(row, 1), :] = hi
            new.append(ar * hr - ai * hi + sre_ref[pl.ds(row, 1), :])
            new.append(ar * hi + ai * hr + sim_ref[pl.ds(row, 1), :])
        return tuple(new)

    fin = lax.fori_loop(0, per_seq, body, (zero,) * (2 * n_seq))
    for q in range(n_seq):
        fre_ref[pl.ds(q, 1), :] = fin[2 * q]
        fim_ref[pl.ds(q, 1), :] = fin[2 * q + 1]


def _ssm_scan(s_re, s_im, a_re, a_im, n_seq):
    rows, width = s_re.shape
    wblk = 1024
    full = jax.ShapeDtypeStruct((rows, width), F32)
    fin = jax.ShapeDtypeStruct((n_seq, width), F32)
    col = lambda i: (0, i)
    return pl.pallas_call(
        functools.partial(_ssm_scan_body, per_seq=rows // n_seq),
        grid=(width // wblk,),
        in_specs=[pl.BlockSpec((rows, wblk), col)] * 2 + [pl.BlockSpec((1, wblk), col)] * 2,
        out_specs=[pl.BlockSpec((rows, wblk), col)] * 2 + [pl.BlockSpec((n_seq, wblk), col)] * 2,
        out_shape=[full, full, fin, fin],
        compiler_params=_params("parallel"),
        name="ssm_scan",
    )(s_re, s_im, a_re, a_im)


def _ssm_out_body(u_ref, t_ref, f_ref, d_ref, hre_ref, him_ref, g_ref):
    cb = hre_ref.shape[0]
    u = _chunk_rows(u_ref, cb)
    y = jnp.dot(u.astype(BF16), t_ref[...], preferred_element_type=F32)
    y = y + jnp.dot(hre_ref[...].astype(BF16), f_ref[0], preferred_element_type=F32)
    y = y + jnp.dot(him_ref[...].astype(BF16), f_ref[1], preferred_element_type=F32)
    g = jax.nn.gelu(y + d_ref[...] * u)
    for t in range(SSM_CHUNK):
        g_ref[pl.ds(t, cb, stride=SSM_CHUNK), :] = g[:, t * LANES:(t + 1) * LANES]


def _ssm_out(u, t_slab, f_slab, d_slab, h_re, h_im, cb):
    t, e = u.shape
    ns, kdim, _ = t_slab.shape
    half = f_slab.shape[2]
    rows = cb * SSM_CHUNK
    return pl.pallas_call(
        _ssm_out_body,
        grid=(ns, t // rows),
        in_specs=[pl.BlockSpec((rows, LANES), lambda a, i: (i, a)),
                  pl.BlockSpec((None, kdim, kdim), lambda a, i: (a, 0, 0)),
                  pl.BlockSpec((None, 2, half, kdim), lambda a, i: (a, 0, 0, 0)),
                  pl.BlockSpec((None, 1, kdim), lambda a, i: (a, 0, 0)),
                  pl.BlockSpec((cb, half), lambda a, i: (i, a)),
                  pl.BlockSpec((cb, half), lambda a, i: (i, a))],
        out_specs=pl.BlockSpec((rows, LANES), lambda a, i: (i, a)),
        out_shape=jax.ShapeDtypeStruct((t, e), F32),
        compiler_params=_params("parallel", "parallel"),
        name="ssm_out",
    )(u, t_slab, f_slab, d_slab, h_re, h_im)


def _ssm_step_body(u_ref, hr_ref, hi_ref, lr_ref, li_ref, wb_ref, cr_ref, ci_ref, d_ref,
                   g_ref, nr_ref, ni_ref):
    u = u_ref[...]
    bu = jnp.dot(u.astype(BF16), wb_ref[...].astype(BF16), preferred_element_type=F32)
    half = bu.shape[1] // 2
    lr, li, hr, hi = lr_ref[...], li_ref[...], hr_ref[...], hi_ref[...]
    nr = lr * hr - li * hi + bu[:, :half]
    ni = lr * hi + li * hr + bu[:, half:]
    nr_ref[...] = nr
    ni_ref[...] = ni
    y = (jnp.dot(nr.astype(BF16), cr_ref[...].astype(BF16), preferred_element_type=F32)
         - jnp.dot(ni.astype(BF16), ci_ref[...].astype(BF16), preferred_element_type=F32)
         + d_ref[...] * u)
    g_ref[...] = jax.nn.gelu(y).astype(BF16)


def _ssm_step(u, h_re, h_im, lb_re, lb_im, bb, c_re, c_im, d):
    bd, e = u.shape
    g, ch, p2 = bb.shape
    p = p2 // 2
    gs = LANES // ch
    ns = g // gs
    eye = jnp.eye(gs, dtype=F32)
    wb = jnp.einsum("sgcrp,gh->sgcrhp", bb.reshape(ns, gs, ch, 2, p), eye).reshape(ns, LANES, 2 * gs * p)
    crb = jnp.einsum("sgcp,gh->shpgc", c_re.reshape(ns, gs, ch, p), eye).reshape(ns, gs * p, LANES)
    cib = jnp.einsum("sgcp,gh->shpgc", c_im.reshape(ns, gs, ch, p), eye).reshape(ns, gs * p, LANES)
    st = jax.ShapeDtypeStruct((bd, g * p), F32)
    col = lambda i: (0, i)
    slab = lambda i: (i, 0, 0)
    return pl.pallas_call(
        _ssm_step_body,
        grid=(ns,),
        in_specs=[pl.BlockSpec((bd, LANES), col),
                  pl.BlockSpec((bd, gs * p), col), pl.BlockSpec((bd, gs * p), col),
                  pl.BlockSpec((1, gs * p), col), pl.BlockSpec((1, gs * p), col),
                  pl.BlockSpec((None, LANES, 2 * gs * p), slab),
                  pl.BlockSpec((None, gs * p, LANES), slab), pl.BlockSpec((None, gs * p, LANES), slab),
                  pl.BlockSpec((1, LANES), col)],
        out_specs=[pl.BlockSpec((bd, LANES), col),
                   pl.BlockSpec((bd, gs * p), col), pl.BlockSpec((bd, gs * p), col)],
        out_shape=[jax.ShapeDtypeStruct((bd, e), BF16), st, st],
        compiler_params=_params("parallel"),
        name="ssm_step",
    )(u, h_re, h_im, lb_re, lb_im, wb, crb, cib, d.reshape(1, g * ch))


def _prompt_proj(xp, pend, *args):
    outs = _proj(xp, *args, fuse=pend)
    if pend is not None:
        return outs[-1], outs[:-1]
    return xp, outs


def _diff_layer(xp, pend, xs, k_pool, v_pool, j, page_table, g, w_in, lam_vec, subln_g, w_out, lam_init, final_g,
                seq, cache):
    wb = w_in.astype(BF16)
    wo = w_out.astype(BF16)
    lam = lam_vec.astype(F32)
    b = xp.shape[0] // seq
    e = xp.shape[1]
    past = page_table.shape[1] * PAGE_SIZE
    xp, (qbp, kp, kbp, vp, vtp, zp) = _prompt_proj(xp, pend, g, wb, "diff", _rope_tables(jnp.arange(seq)), True,
                                                   cache)
    qs, ks, vs, zs = _proj(xs, g, wb, "diff", _rope_tables(jnp.full((xs.shape[0],), past)), False)
    r3 = lambda a: a.reshape(b, seq, e)
    op = _attn_prompt(r3(qbp), r3(kbp), vtp, "diff", (lam,), lam_init, seq).reshape(b * seq, e)
    os_ = _decode_attn(qs, ks, vs, k_pool, v_pool, page_table, "diff", (lam,), lam_init, j)
    pend = None
    if final_g is not None:
        xp = _oproj(op, zp, xp, wo, "diff", subln_g=subln_g, lam_init=lam_init, final_g=final_g)
    else:
        pend = (op, zp, wo, "diff", subln_g.reshape(1, LANES), lam_init)
    xs = _oproj(os_, zs, xs, wo, "diff", subln_g=subln_g, lam_init=lam_init, final_g=final_g)
    return xp, pend, xs, kp, vp, ks, vs


def _fox_layer(xp, pend, xs, k_pool, v_pool, lf_pool, j, page_table, g, w_in, b_f, w_out, seq, cache):
    nh = b_f.shape[0]
    wb = jnp.pad(w_in, ((0, 0), (0, LANES - nh))).astype(BF16)
    wo = w_out.astype(BF16)
    bf = b_f.reshape(1, nh).astype(F32)
    b = xp.shape[0] // seq
    e = xp.shape[1]
    xp, (qbp, kp, kbp, vp, vtp, zp, lfp) = _prompt_proj(xp, pend, g, wb, "fox", (bf,), True, cache)
    qs, ks, vs, zs, lfs = _proj(xs, g, wb, "fox", (bf,), False)
    r3 = lambda a: a.reshape(b, seq, e)
    bias = _fox_bias(lfp.reshape(b, seq, nh))
    op = _attn_prompt(r3(qbp), r3(kbp), vtp, "fox", (bias,), 0.0, seq).reshape(b * seq, e)
    os_ = _decode_attn(qs, ks, vs, k_pool, v_pool, page_table, "fox", (lfs.T, lf_pool), 0.0, j)
    xs = _oproj(os_, zs, xs, wo, "fox")
    return xp, (op, zp, wo, "fox", None, 0.0), xs, kp, vp, lfp, ks, vs, lfs


def _ssm_layer(xp, pend, xs, h0_re, h0_im, g, w_in, a_re, a_im, log_step, b_re, b_im, c_re, c_im, d, w_glu, w_out, seq):
    ng, p = a_re.shape
    ch, n = SSM_CH, SSM_CHUNK
    wb = w_in.astype(BF16)
    wg = w_glu.astype(BF16)
    wo = w_out.astype(BF16)
    b = xp.shape[0] // seq
    e = xp.shape[1]
    pw_re, pw_im, f_re, f_im = _ssm_discretize(a_re, a_im, log_step)
    t_tab, x_tab, f_tab, bb = _ssm_tables(pw_re, pw_im, f_re, f_im, b_re, b_im, c_re, c_im)
    dd = d.astype(F32)
    flat = lambda a: a[:, :p].reshape(1, ng * p)
    gs = LANES // ch
    ns = ng // gs
    t_slab = _slab_expand(t_tab, ns, gs, ch)
    e_slab = _slab_expand(x_tab, ns, gs, p)
    f_slab = jnp.swapaxes(_slab_expand(f_tab, ns, gs, p), 1, 2).reshape(ns, 2, gs * p, n * LANES)
    d_slab = jnp.broadcast_to(dd.reshape(ns, 1, gs * ch), (ns, n, gs * ch)).reshape(ns, 1, n * LANES)
    xp, (up, zp) = _prompt_proj(xp, pend, g, wb, "ssm", (), True)
    nchunk = b * seq // n
    cb = min(SSM_CHUNK_ROWS, nchunk)
    s_re, s_im = _ssm_local(up, e_slab, cb)
    h_re, h_im, fin_re, fin_im = _ssm_scan(s_re, s_im, flat(pw_re[n]), flat(pw_im[n]), b)
    gp = _ssm_out(up, t_slab, f_slab, d_slab, h_re, h_im, cb)
    xp = _oproj(gp, zp, xp, wo, "ssm", wg=wg)
    us, zs = _proj(xs, g, wb, "ssm", (), False)
    bd = xs.shape[0]
    gs, ns_re, ns_im = _ssm_step(us, h0_re.reshape(bd, ng * p).astype(F32), h0_im.reshape(bd, ng * p).astype(F32),
                                 flat(pw_re[1]), flat(pw_im[1]), bb, c_re.astype(F32), c_im.astype(F32), dd)
    xs = _oproj(gs, zs, xs, wo, "ssm", wg=wg)
    return xp, xs, fin_re.reshape(b, ng, p), fin_im.reshape(b, ng, p), ns_re.reshape(bd, ng, p), ns_im.reshape(bd, ng, p)


def kernel(x_prompt, x_sample, cache_diff_k, cache_diff_v, cache_fox_k, cache_fox_v, cache_fox_logf, state_ssm_re, state_ssm_im, page_table, norm_g, final_norm_g, diff_w_in, diff_lambda, diff_subln_g, diff_w_out, fox_w_in, fox_b_f, fox_w_out, ssm_w_in, ssm_a_re, ssm_a_im, ssm_log_step, ssm_b_re, ssm_b_im, ssm_c_re, ssm_c_im, ssm_d, ssm_w_glu, ssm_w_out):
    b, seq, dm = x_prompt.shape
    bd = x_sample.shape[0]
    depth = norm_g.shape[0]
    xp = x_prompt.reshape(b * seq, dm)
    xs = x_sample.reshape(bd, dm)
    rows = lambda c: c.reshape(c.shape[0], c.shape[1], c.shape[2] * c.shape[3], c.shape[4])
    cols = lambda c: jnp.transpose(c, (0, 1, 3, 4, 2)).reshape(c.shape[0], c.shape[1], c.shape[3] * c.shape[4],
                                                                c.shape[2])
    dk_pool, dv_pool = rows(cache_diff_k), rows(cache_diff_v)
    fk_pool, fv_pool = cols(cache_fox_k), cols(cache_fox_v)
    fl_pool = jnp.transpose(cache_fox_logf, (0, 1, 3, 2))
    acc = {name: [] for name in ("dks", "dvs", "flp", "fks", "fvs", "fls", "srp", "sip", "srs", "sis")}
    ha, hb = cache_diff_k.shape[3], cache_fox_k.shape[3]
    diff_kv = fox_kv = None
    pend = None
    for i in range(depth):
        kind, j = i % N_MIXERS, i // N_MIXERS
        final_g = final_norm_g if i == depth - 1 else None
        if kind == 0:
            xp, pend, xs, kp, vp, ks, vs = _diff_layer(
                xp, pend, xs, dk_pool, dv_pool, j, page_table, norm_g[i], diff_w_in[j], diff_lambda[j],
                diff_subln_g[j], diff_w_out[j], _diff_lambda_init(i), final_g, seq,
                (j, diff_w_in.shape[0], diff_kv, b))
            diff_kv = (kp, vp)
            acc["dks"].append(ks); acc["dvs"].append(vs)
        elif kind == 1:
            xp, pend, xs, kp, vp, lfp, ks, vs, lfs = _fox_layer(
                xp, pend, xs, fk_pool, fv_pool, fl_pool, j, page_table, norm_g[i], fox_w_in[j], fox_b_f[j],
                fox_w_out[j], seq, (j, fox_w_in.shape[0], fox_kv, b))
            fox_kv = (kp, vp)
            acc["flp"].append(lfp)
            acc["fks"].append(ks); acc["fvs"].append(vs); acc["fls"].append(lfs)
        else:
            xp, xs, hrp, hip, hrs, his = _ssm_layer(
                xp, pend, xs, state_ssm_re[j], state_ssm_im[j], norm_g[i], ssm_w_in[j], ssm_a_re[j], ssm_a_im[j],
                ssm_log_step[j], ssm_b_re[j], ssm_b_im[j], ssm_c_re[j], ssm_c_im[j], ssm_d[j], ssm_w_glu[j],
                ssm_w_out[j], seq)
            pend = None
            acc["srp"].append(hrp); acc["sip"].append(hip); acc["srs"].append(hrs); acc["sis"].append(his)
    stack = lambda name, shape: jnp.stack(acc[name]).reshape((len(acc[name]),) + shape)
    diff_rows = lambda a: a.reshape(a.shape[0], b, seq, ha, dm // ha)
    fox_rows = lambda a: a.reshape(a.shape[0], b, hb, dm // hb, seq).transpose(0, 1, 4, 2, 3)
    return (xp.reshape(b, seq, dm), xs.reshape(bd, 1, dm),
            diff_rows(diff_kv[0]), diff_rows(diff_kv[1]),
            stack("dks", (bd, 1, ha, dm // ha)), stack("dvs", (bd, 1, ha, dm // ha)),
            fox_rows(fox_kv[0]), fox_rows(fox_kv[1]),
            stack("flp", (b, seq, hb)),
            stack("fks", (bd, 1, hb, dm // hb)), stack("fvs", (bd, 1, hb, dm // hb)),
            stack("fls", (bd, 1, hb)),
            jnp.stack(acc["srp"]), jnp.stack(acc["sip"]), jnp.stack(acc["srs"]), jnp.stack(acc["sis"]))
```
